```python
import math
import jax
import jax.numpy as jnp
from jax import lax
import numpy as np

D_MODEL = 2048
BATCH = 2
SEQ = 8192
DEPTH = 1
DEC_BATCH = 32
DEC_SEQ = 4
PAST_LEN = 16384
PAGE_SIZE = 128

DIFF_WIDTH = D_MODEL // 2
N_DIFF_HEADS = 4
DV_DIFF = DIFF_WIDTH // N_DIFF_HEADS
DH_DIFF = DV_DIFF // 2
Q_BLOCK = 128
GLA_WIDTH = D_MODEL - DIFF_WIDTH
N_GLA_HEADS = 4
DV_GLA = GLA_WIDTH // N_GLA_HEADS
DK_GLA = DV_GLA // 2
GATE_RANK = 16
GATE_TAU = 16.0
GLA_CHUNK = 64
MEM_LEN = 256
N_MEM_HEADS = 4
DH_MEM = D_MODEL // N_MEM_HEADS
N_GROUPS = 4
EXPERTS_PER_GROUP = 8
N_EXPERTS = N_GROUPS * EXPERTS_PER_GROUP
TOP_K_IN_GROUP = 2
D_EXPERT = D_MODEL // 4
RMS_EPS = 1e-6
IN_SPLITS = (N_DIFF_HEADS * 2 * DH_DIFF, N_DIFF_HEADS * 2 * DH_DIFF, DIFF_WIDTH,
             N_GLA_HEADS * DK_GLA, N_GLA_HEADS * DK_GLA, GLA_WIDTH, GLA_WIDTH, GATE_RANK)
D_IN = sum(IN_SPLITS)

kernel_name = 'hymba_diffattn_gla_hmoe_step'


def rmsnorm(x, g):
    xf = x.astype(jnp.float32)
    y = xf * lax.rsqrt(jnp.mean(xf * xf, axis=-1, keepdims=True) + RMS_EPS)
    return (y * g.astype(jnp.float32)).astype(x.dtype)


def project_mixers(xn, w_in, w_a2, b_a):
    B, T, _ = xn.shape
    h = xn @ w_in
    parts = []
    off = 0
    for w in IN_SPLITS:
        parts.append(h[..., off:off + w])
        off += w
    dq, dk, dv, gq, gk, gv, gr, ga = parts
    dq = dq.reshape(B, T, N_DIFF_HEADS, 2, DH_DIFF)
    dk = dk.reshape(B, T, N_DIFF_HEADS, 2 * DH_DIFF)
    dv = dv.reshape(B, T, N_DIFF_HEADS, DV_DIFF)
    gq = gq.reshape(B, T, N_GLA_HEADS, DK_GLA) * (DK_GLA ** -0.5)
    gk = gk.reshape(B, T, N_GLA_HEADS, DK_GLA)
    gv = gv.reshape(B, T, N_GLA_HEADS, DV_GLA)
    glog = jax.nn.log_sigmoid((ga @ w_a2 + b_a).astype(jnp.float32)) / GATE_TAU
    glog = glog.reshape(B, T, N_GLA_HEADS, DK_GLA)
    return dq, dk, dv, gq, gk, gv, gr, glog


def diff_attend(q, k, v, mask, lam):
    k = k.reshape(*k.shape[:-1], 2, DH_DIFF)
    s = jnp.einsum('bqhmd,bkhmd->bhmqk', q, k).astype(jnp.float32) * (DH_DIFF ** -0.5)
    s = jnp.where(mask, s, -jnp.inf)
    p = jax.nn.softmax(s, axis=-1)
    a = p[:, :, 0] - lam * p[:, :, 1]
    return jnp.einsum('bhqk,bkhv->bqhv', a.astype(v.dtype), v)


def diff_prompt(q, k, v, lam):
    B, T = q.shape[:2]
    qb = min(Q_BLOCK, T)
    nb = T // qb
    kpos = jnp.arange(T)

    def block(i):
        q_i = lax.dynamic_slice_in_dim(q, i * qb, qb, axis=1)
        qpos = i * qb + jnp.arange(qb)
        mask = kpos[None, :] <= qpos[:, None]
        return diff_attend(q_i, k, v, mask, lam)

    o = lax.map(block, jnp.arange(nb))
    return jnp.moveaxis(o, 0, 1).reshape(B, T, N_DIFF_HEADS, DV_DIFF)


def diff_sample(q, k, v, lam, cache_k, cache_v, page_table):
    T = q.shape[1]
    past = page_table.shape[1] * cache_k.shape[1]
    mask = jnp.concatenate([jnp.ones((T, past), bool), jnp.tril(jnp.ones((T, T), bool))], axis=1)

    def per_seq(args):
        pt, q_b, k_b, v_b = args
        kp = cache_k[pt].reshape(past, N_DIFF_HEADS, 2 * DH_DIFF)
        vp = cache_v[pt].reshape(past, N_DIFF_HEADS, DV_DIFF)
        k_all = jnp.concatenate([kp, k_b], axis=0)[None]
        v_all = jnp.concatenate([vp, v_b], axis=0)[None]
        return diff_attend(q_b[None], k_all, v_all, mask, lam)[0]

    return lax.map(per_seq, (page_table, q, k, v))


def gla_scan(q, k, v, glog, s0):
    B, T, H, DK = q.shape
    DV = v.shape[-1]
    c = GLA_CHUNK if T % GLA_CHUNK == 0 else T
    n = T // c

    def chunks(a):
        return jnp.moveaxis(a.astype(jnp.float32).reshape(B, n, c, *a.shape[2:]), 1, 0)

    causal = jnp.tril(jnp.ones((c, c), bool))

    def step(S, inp):
        qc, kc, vc, gc = inp
        b = jnp.cumsum(gc, axis=1)
        o_inter = jnp.einsum('bthk,bhkv->bthv', qc * jnp.exp(b), S)
        rel = jnp.where(causal[None, :, :, None, None], b[:, :, None] - b[:, None, :], -jnp.inf)
        att = jnp.sum(qc[:, :, None] * kc[:, None] * jnp.exp(rel), axis=-1)
        o_intra = jnp.einsum('btsh,bshv->bthv', att, vc)
        b_last = b[:, -1]
        S = jnp.exp(b_last)[..., None] * S + jnp.einsum('bshk,bshv->bhkv', kc * jnp.exp(b_last[:, None] - b), vc)
        return S, o_inter + o_intra

    S, o = lax.scan(step, s0.astype(jnp.float32), (chunks(q), chunks(k), chunks(v), chunks(glog)))
    return jnp.moveaxis(o, 0, 1).reshape(B, T, H, DV), S


def mem_kv(mem, g_mem, w_k, w_v):
    B, M, _ = mem.shape
    mn = rmsnorm(mem, g_mem)
    return ((mn @ w_k).reshape(B, M, N_MEM_HEADS, DH_MEM), (mn @ w_v).reshape(B, M, N_MEM_HEADS, DH_MEM))


def cross_attend(xn, mk, mv, w_q, w_o):
    B, T, D = xn.shape
    q = (xn @ w_q).reshape(B, T, N_MEM_HEADS, DH_MEM)
    s = jnp.einsum('bthd,bmhd->bhtm', q, mk).astype(jnp.float32) * (DH_MEM ** -0.5)
    p = jax.nn.softmax(s, axis=-1)
    o = jnp.einsum('bhtm,bmhd->bthd', p.astype(mv.dtype), mv).reshape(B, T, D)
    return o @ w_o


def hier_moe(xn, w_group, w_router, w_gate, w_up, w_down):
    B, T, D = xn.shape
    xt = xn.reshape(-1, D)
    lg = (xt @ w_group).astype(jnp.float32)
    pg = jax.nn.softmax(lg, axis=-1)
    g_idx = jnp.argmax(lg, axis=-1)
    g_w = jnp.take_along_axis(pg, g_idx[:, None], axis=-1)
    le = (xt @ w_router).astype(jnp.float32).reshape(-1, N_GROUPS, EXPERTS_PER_GROUP)
    le_sel = jnp.take_along_axis(le, g_idx[:, None, None], axis=1)[:, 0]
    top_v, top_i = lax.top_k(le_sel, TOP_K_IN_GROUP)
    e_w = jax.nn.softmax(top_v, axis=-1) * g_w
    e_idx = g_idx[:, None] * EXPERTS_PER_GROUP + top_i
    gates = jnp.sum(jax.nn.one_hot(e_idx, N_EXPERTS, dtype=jnp.float32) * e_w[..., None], axis=1)
    gates = gates.astype(xt.dtype)
    y = jnp.zeros_like(xt)
    for e in range(N_EXPERTS):
        h = jax.nn.silu(xt @ w_gate[e]) * (xt @ w_up[e])
        y = y + gates[:, e:e + 1] * (h @ w_down[e])
    return y.reshape(B, T, D)


def trunk_layer(x, diff_fn, s0, mk, mv, lam_init, g_mix, w_in, w_a2, b_a, lq1, lk1, lq2, lk2,
                g_sub, g_gla, w_o, g_cross, w_q_mem, w_o_mem, g_ffn, w_group, w_router, w_gate, w_up, w_down):
    B, T = x.shape[:2]
    xn = rmsnorm(x, g_mix)
    dq, dk, dv, gq, gk, gv, gr, glog = project_mixers(xn, w_in, w_a2, b_a)
    lam = (jnp.exp(jnp.sum(lq1 * lk1).astype(jnp.float32))
           - jnp.exp(jnp.sum(lq2 * lk2).astype(jnp.float32)) + lam_init)
    d_out = rmsnorm(diff_fn(dq, dk, dv, lam), g_sub) * (1.0 - lam_init)
    g_o, S = gla_scan(gq, gk, gv, glog, s0)
    g_out = rmsnorm(g_o.astype(x.dtype), g_gla).reshape(B, T, GLA_WIDTH) * jax.nn.silu(gr)
    x = x + jnp.concatenate([d_out.reshape(B, T, DIFF_WIDTH), g_out], axis=-1) @ w_o
    x = x + cross_attend(rmsnorm(x, g_cross), mk, mv, w_q_mem, w_o_mem)
    x = x + hier_moe(rmsnorm(x, g_ffn), w_group, w_router, w_gate, w_up, w_down)
    return x, dk, dv, S.astype(s0.dtype)


def setup_inputs(seed: int = 0) -> dict:
    key = jax.random.key(seed)
    keys = jax.random.split(key, 40)
    cnt = [0]

    def nrm(shape, scale=1.0):
        k = keys[cnt[0]]
        cnt[0] += 1
        return jax.random.normal(k, shape, jnp.float32) * scale

    def gain(shape):
        return 1.0 + nrm(shape, 0.02)

    n_pages = PAST_LEN // PAGE_SIZE
    n_used = DEC_BATCH * n_pages
    n_pool = n_used + max(1, n_used // 4)
    L, D = DEPTH, D_MODEL
    sd = D ** -0.5
    inp = {}
    inp['x_prompt'] = nrm((BATCH, SEQ, D))
    inp['x_sample'] = nrm((DEC_BATCH, DEC_SEQ, D))
    inp['mem_prompt'] = nrm((BATCH, MEM_LEN, D))
    inp['cache_k'] = nrm((L, n_pool, PAGE_SIZE, N_DIFF_HEADS, 2 * DH_DIFF))
    inp['cache_v'] = nrm((L, n_pool, PAGE_SIZE, N_DIFF_HEADS, DV_DIFF))
    inp['state_gla'] = nrm((L, DEC_BATCH, N_GLA_HEADS, DK_GLA, DV_GLA))
    inp['cache_mem_k'] = nrm((L, DEC_BATCH, MEM_LEN, N_MEM_HEADS, DH_MEM))
    inp['cache_mem_v'] = nrm((L, DEC_BATCH, MEM_LEN, N_MEM_HEADS, DH_MEM))
    inp['page_table'] = jax.random.permutation(keys[39], n_pool)[:n_used].reshape(DEC_BATCH, n_pages).astype(jnp.int32)
    inp['g_mix'] = gain((L, D))
    inp['w_in'] = nrm((L, D, D_IN), sd)
    inp['w_a2'] = nrm((L, GATE_RANK, N_GLA_HEADS * DK_GLA), GATE_RANK ** -0.5)
    inp['b_a'] = nrm((L, N_GLA_HEADS * DK_GLA), 0.1)
    inp['lambda_q1'] = nrm((L, DH_DIFF), 0.1)
    inp['lambda_k1'] = nrm((L, DH_DIFF), 0.1)
    inp['lambda_q2'] = nrm((L, DH_DIFF), 0.1)
    inp['lambda_k2'] = nrm((L, DH_DIFF), 0.1)
    inp['g_subln'] = gain((L, DV_DIFF))
    inp['g_gla'] = gain((L, DV_GLA))
    inp['w_o'] = nrm((L, D, D), sd)
    inp['g_mem'] = gain((L, D))
    inp['g_cross'] = gain((L, D))
    inp['w_q_mem'] = nrm((L, D, D), sd)
    inp['w_k_mem'] = nrm((L, D, D), sd)
    inp['w_v_mem'] = nrm((L, D, D), sd)
    inp['w_o_mem'] = nrm((L, D, D), sd)
    inp['g_ffn'] = gain((L, D))
    inp['w_group'] = nrm((L, D, N_GROUPS), sd)
    inp['w_router'] = nrm((L, D, N_EXPERTS), sd)
    inp['w_gate'] = nrm((L, N_EXPERTS, D, D_EXPERT), sd)
    inp['w_up'] = nrm((L, N_EXPERTS, D, D_EXPERT), sd)
    inp['w_down'] = nrm((L, N_EXPERTS, D_EXPERT, D), D_EXPERT ** -0.5)
    inp['g_final'] = gain((D,))
    return inp


def reference(x_prompt, x_sample, mem_prompt, cache_k, cache_v, state_gla, cache_mem_k, cache_mem_v,
              page_table, g_mix, w_in, w_a2, b_a, lambda_q1, lambda_k1, lambda_q2, lambda_k2, g_subln,
              g_gla, w_o, g_mem, g_cross, w_q_mem, w_k_mem, w_v_mem, w_o_mem, g_ffn, w_group, w_router,
              w_gate, w_up, w_down, g_final):
    xp, xs = x_prompt, x_sample
    kp_l, vp_l, sp_l, mkp_l, mvp_l, ks_l, vs_l, ss_l = [], [], [], [], [], [], [], []
    s0_prompt = jnp.zeros((x_prompt.shape[0], N_GLA_HEADS, DK_GLA, DV_GLA), state_gla.dtype)
    for l in range(DEPTH):
        lam_init = 0.8 - 0.6 * math.exp(-0.3 * l)
        lw = (g_mix[l], w_in[l], w_a2[l], b_a[l], lambda_q1[l], lambda_k1[l], lambda_q2[l], lambda_k2[l],
              g_subln[l], g_gla[l], w_o[l], g_cross[l], w_q_mem[l], w_o_mem[l], g_ffn[l], w_group[l],
              w_router[l], w_gate[l], w_up[l], w_down[l])
        mkp, mvp = mem_kv(mem_prompt, g_mem[l], w_k_mem[l], w_v_mem[l])
        xp, kp, vp, sp = trunk_layer(xp, diff_prompt, s0_prompt, mkp, mvp, lam_init, *lw)
        ck, cv = cache_k[l], cache_v[l]
        sample_fn = lambda q, k, v, lam, ck=ck, cv=cv: diff_sample(q, k, v, lam, ck, cv, page_table)
        xs, ks, vs, ss = trunk_layer(xs, sample_fn, state_gla[l], cache_mem_k[l], cache_mem_v[l], lam_init, *lw)
        kp_l.append(kp); vp_l.append(vp); sp_l.append(sp); mkp_l.append(mkp); mvp_l.append(mvp)
        ks_l.append(ks); vs_l.append(vs); ss_l.append(ss)
    y_prompt = rmsnorm(xp, g_final)
    y_sample = rmsnorm(xs, g_final)
    return (y_prompt, y_sample, jnp.stack(kp_l), jnp.stack(vp_l), jnp.stack(sp_l), jnp.stack(mkp_l),
            jnp.stack(mvp_l), jnp.stack(ks_l), jnp.stack(vs_l), jnp.stack(ss_l))
```

```python
import functools
import math

import jax
import jax.numpy as jnp
from jax import lax
from jax.experimental import pallas as pl
from jax.experimental.pallas import tpu as pltpu

F32 = jnp.float32
BF16 = jnp.bfloat16
I32 = jnp.int32

N_DIFF_HEADS = 4
N_GLA_HEADS = 4
GATE_RANK = 16
GATE_TAU = 16.0
N_MEM_HEADS = 4
N_GROUPS = 4
EXPERTS_PER_GROUP = 8
N_EXPERTS = N_GROUPS * EXPERTS_PER_GROUP
RMS_EPS = 1e-6
GLA_SUB = 8

LANE = 128
SUBLANE = 8
MIB = 1024 * 1024

_NT = (((1,), (1,)), ((), ()))
_TN = (((0,), (0,)), ((), ()))


def _cparams(semantics, vmem_mib):
    return pltpu.CompilerParams(dimension_semantics=semantics, vmem_limit_bytes=int(vmem_mib * MIB))


def _tile(n, target):
    if n <= target:
        return n
    t = target - target % SUBLANE
    while t >= SUBLANE:
        if n % t == 0:
            return t
        t -= SUBLANE
    raise ValueError(f"no sublane-aligned tile of {n} below {target}")


def _rms(x, g):
    ms = jnp.mean(x * x, axis=-1, keepdims=True)
    return x * lax.rsqrt(ms + RMS_EPS) * g


def _sigmoid(x):
    return 1.0 / (1.0 + jnp.exp(-x))


def _inproj_kernel(x_ref, g_ref, w_ref, wga_ref, wa2_ref, ba_ref,
                   q_ref, k_ref, v_ref, gqk_ref, gv_ref, gr_ref, glog_ref, xn_ref):
    j = pl.program_id(1)

    @pl.when(j == 0)
    def _norm():
        xn_ref[...] = _rms(x_ref[...], g_ref[...]).astype(BF16)

    outs = (q_ref, k_ref, v_ref, gqk_ref, gv_ref, gr_ref)
    for idx, o_ref in enumerate(outs):
        @pl.when(j == idx)
        def _proj(o_ref=o_ref):
            o_ref[...] = jnp.dot(xn_ref[...], w_ref[...], preferred_element_type=F32)

    @pl.when(j == len(outs))
    def _gate():
        ga = jnp.dot(xn_ref[...], wga_ref[...], preferred_element_type=F32)
        z = jnp.dot(ga.astype(BF16), wa2_ref[...], preferred_element_type=F32) + ba_ref[...]
        glog_ref[...] = (jnp.minimum(z, 0.0) - jnp.log1p(jnp.exp(-jnp.abs(z)))) * (1.0 / GATE_TAU)


def _in_projection(x, g_mix, w_main, w_ga, w_a2p, b_a):
    n, d = x.shape
    wblk = d // 2
    n_groups = w_main.shape[1] // wblk
    gk = w_a2p.shape[1]
    tm = _tile(n, 512)
    grid = (n // tm, n_groups + 1)
    row = lambda i, j: (i, 0)
    fixed = lambda i, j: (0, 0)
    out_shape = [jax.ShapeDtypeStruct((n, wblk), F32)] * n_groups + [jax.ShapeDtypeStruct((n, gk), F32)]
    out_specs = [pl.BlockSpec((tm, wblk), row)] * n_groups + [pl.BlockSpec((tm, gk), row)]
    return pl.pallas_call(
        _inproj_kernel,
        grid=grid,
        in_specs=[
            pl.BlockSpec((tm, d), row),
            pl.BlockSpec((1, d), fixed),
            pl.BlockSpec((d, wblk), lambda i, j: (0, jnp.minimum(j, n_groups - 1))),
            pl.BlockSpec((d, LANE), fixed),
            pl.BlockSpec((LANE, gk), fixed),
            pl.BlockSpec((1, gk), fixed),
        ],
        out_specs=out_specs,
        out_shape=out_shape,
        scratch_shapes=[pltpu.VMEM((tm, d), BF16)],
        compiler_params=_cparams(("parallel", "arbitrary"), 56),
        name="in_projection",
    )(x, g_mix, w_main, w_ga, w_a2p, b_a)


def _norm_mm_kernel(x_ref, g_ref, w_ref, o_ref, xn_ref):
    @pl.when(pl.program_id(1) == 0)
    def _norm():
        xn_ref[...] = _rms(x_ref[...], g_ref[...]).astype(BF16)

    o_ref[...] = jnp.dot(xn_ref[...], w_ref[...], preferred_element_type=F32).astype(o_ref.dtype)


def _norm_matmul(x, g, w, out_dtype, name):
    n, d = x.shape
    dout = w.shape[1]
    tm = _tile(n, 512)
    tn = _tile(dout, 1024)
    return pl.pallas_call(
        _norm_mm_kernel,
        grid=(n // tm, dout // tn),
        in_specs=[
            pl.BlockSpec((tm, d), lambda i, j: (i, 0)),
            pl.BlockSpec((1, d), lambda i, j: (0, 0)),
            pl.BlockSpec((d, tn), lambda i, j: (0, j)),
        ],
        out_specs=pl.BlockSpec((tm, tn), lambda i, j: (i, j)),
        out_shape=jax.ShapeDtypeStruct((n, dout), out_dtype),
        scratch_shapes=[pltpu.VMEM((tm, d), BF16)],
        compiler_params=_cparams(("parallel", "arbitrary"), 40),
        name=name,
    )(x, g, w)


def _mm_res_kernel(*refs, widths):
    a_refs = refs[:len(widths)]
    w_ref, res_ref, o_ref = refs[len(widths):]
    acc = res_ref[...]
    k0 = 0
    for a_ref, kw in zip(a_refs, widths):
        acc = acc + jnp.dot(a_ref[...].astype(BF16), w_ref[k0:k0 + kw, :], preferred_element_type=F32)
        k0 += kw
    o_ref[...] = acc


def _matmul_residual(pieces, w, res, name):
    n, dout = res.shape
    widths = tuple(p.shape[1] for p in pieces)
    assert sum(widths) == w.shape[0]
    tm = _tile(n, 512)
    tn = _tile(dout, 1024)
    in_specs = [pl.BlockSpec((tm, kw), lambda i, j: (i, 0)) for kw in widths]
    in_specs += [pl.BlockSpec((w.shape[0], tn), lambda i, j: (0, j)),
                 pl.BlockSpec((tm, tn), lambda i, j: (i, j))]
    return pl.pallas_call(
        functools.partial(_mm_res_kernel, widths=widths),
        grid=(n // tm, dout // tn),
        in_specs=in_specs,
        out_specs=pl.BlockSpec((tm, tn), lambda i, j: (i, j)),
        out_shape=jax.ShapeDtypeStruct((n, dout), F32),
        compiler_params=_cparams(("parallel", "arbitrary"), 40),
        name=name,
    )(*pieces, w, res)


def _lambda(lq1_ref, lk1_ref, lq2_ref, lk2_ref, lam_init):
    d1 = jnp.sum(lq1_ref[...] * lk1_ref[...], axis=-1, keepdims=True)
    d2 = jnp.sum(lq2_ref[...] * lk2_ref[...], axis=-1, keepdims=True)
    return jnp.exp(d1) - jnp.exp(d2) + lam_init


def _diff_prompt_kernel(qi_ref, ki_ref, q_ref, k_ref, v_ref, lq1_ref, lk1_ref, lq2_ref, lk2_ref, gsub_ref,
                        o_ref, m_ref, l_ref, acc_ref, *, dh, scale, lam_init):
    p = pl.program_id(2)
    qi = qi_ref[p]
    ki = ki_ref[p]

    @pl.when(ki == 0)
    def _init():
        m_ref[...] = jnp.full(m_ref.shape, -jnp.inf, F32)
        l_ref[...] = jnp.zeros(l_ref.shape, F32)
        acc_ref[...] = jnp.zeros(acc_ref.shape, F32)

    def update(masked):
        v = v_ref[...].astype(BF16)
        tq, tk = q_ref.shape[0], k_ref.shape[0]
        if masked:
            keep = lax.broadcasted_iota(I32, (tq, tk), 1) <= lax.broadcasted_iota(I32, (tq, tk), 0)
        for mi in range(2):
            qm = (q_ref[:, mi * dh:(mi + 1) * dh] * scale).astype(BF16)
            km = k_ref[:, mi * dh:(mi + 1) * dh].astype(BF16)
            s = lax.dot_general(qm, km, _NT, preferred_element_type=F32)
            if masked:
                s = jnp.where(keep, s, -jnp.inf)
            m_prev = m_ref[mi]
            m_new = jnp.maximum(m_prev, jnp.max(s, axis=-1, keepdims=True))
            alpha = jnp.exp(m_prev - m_new)
            pr = jnp.exp(s - m_new)
            l_ref[mi] = alpha * l_ref[mi] + jnp.sum(pr, axis=-1, keepdims=True)
            acc_ref[mi] = alpha * acc_ref[mi] + jnp.dot(pr.astype(BF16), v, preferred_element_type=F32)
            m_ref[mi] = m_new

    @pl.when(ki < qi)
    def _below_diagonal():
        update(False)

    @pl.when(ki == qi)
    def _diagonal():
        update(True)
        lam = _lambda(lq1_ref, lk1_ref, lq2_ref, lk2_ref, lam_init)
        o = acc_ref[0] / l_ref[0] - lam * (acc_ref[1] / l_ref[1])
        o_ref[...] = (_rms(o, gsub_ref[...]) * (1.0 - lam_init)).astype(o_ref.dtype)


def _diff_attention_prompt(q, k, v, lams, g_sub, lam_init, batch, seq, out_dtype):
    n, width = q.shape
    dv = width // N_DIFF_HEADS
    dh = dv // 2
    tq = _tile(seq, 512)
    nq = seq // tq
    pairs = [(a, b) for a in range(nq) for b in range(a + 1)]
    qi = jnp.asarray([a for a, _ in pairs], I32)
    ki = jnp.asarray([b for _, b in pairs], I32)
    grid_spec = pltpu.PrefetchScalarGridSpec(
        num_scalar_prefetch=2,
        grid=(batch, N_DIFF_HEADS, len(pairs)),
        in_specs=[
            pl.BlockSpec((tq, dv), lambda b, h, p, qi, ki: (b * nq + qi[p], h)),
            pl.BlockSpec((tq, dv), lambda b, h, p, qi, ki: (b * nq + ki[p], h)),
            pl.BlockSpec((tq, dv), lambda b, h, p, qi, ki: (b * nq + ki[p], h)),
        ] + [pl.BlockSpec((1, dh), lambda b, h, p, qi, ki: (0, 0))] * 4
          + [pl.BlockSpec((1, dv), lambda b, h, p, qi, ki: (0, 0))],
        out_specs=pl.BlockSpec((tq, dv), lambda b, h, p, qi, ki: (b * nq + qi[p], h)),
        scratch_shapes=[pltpu.VMEM((2, tq, 1), F32), pltpu.VMEM((2, tq, 1), F32), pltpu.VMEM((2, tq, dv), F32)],
    )
    return pl.pallas_call(
        functools.partial(_diff_prompt_kernel, dh=dh, scale=dh ** -0.5, lam_init=lam_init),
        grid_spec=grid_spec,
        out_shape=jax.ShapeDtypeStruct((n, width), out_dtype),
        compiler_params=_cparams(("parallel", "parallel", "arbitrary"), 40),
        name="diff_attention_prompt",
    )(qi, ki, q, k, v, *lams, g_sub)


def _diff_decode_kernel(pt_ref, q_ref, kn_ref, vn_ref, lq1_ref, lk1_ref, lq2_ref, lk2_ref, gsub_ref, *rest,
                        n_pg, t_new, dh, scale, lam_init):
    k_refs = rest[:n_pg]
    v_refs = rest[n_pg:2 * n_pg]
    o_ref, m_ref, l_ref, acc_ref = rest[2 * n_pg:]
    g = pl.program_id(1)
    dv = 2 * dh
    rows = 2 * t_new
    page = k_refs[0].shape[0]
    row_q = lax.broadcasted_iota(I32, (rows, dv), 0)
    lane_q = lax.broadcasted_iota(I32, (rows, dv), 1)

    def q_blockdiag(h):
        q = q_ref[0, :, h * dv:(h + 1) * dv] * scale
        q_up = jnp.where((row_q < t_new) & (lane_q < dh), q, 0.0)
        q_dn = jnp.where((row_q >= t_new) & (lane_q >= dh), pltpu.roll(q, t_new, 0), 0.0)
        return (q_up + q_dn).astype(BF16)

    def accumulate(h, s, v_list, first):
        m_new = jnp.max(s, axis=-1, keepdims=True)
        if not first:
            m_prev = m_ref[h]
            m_new = jnp.maximum(m_prev, m_new)
        pr = jnp.exp(s - m_new)
        pv = None
        for jj, vv in enumerate(v_list):
            part = jnp.dot(pr[:, jj * page:(jj + 1) * page].astype(BF16), vv, preferred_element_type=F32)
            pv = part if pv is None else pv + part
        if first:
            l_ref[h] = jnp.sum(pr, axis=-1, keepdims=True)
            acc_ref[h] = pv
        else:
            alpha = jnp.exp(m_prev - m_new)
            l_ref[h] = alpha * l_ref[h] + jnp.sum(pr, axis=-1, keepdims=True)
            acc_ref[h] = alpha * acc_ref[h] + pv
        m_ref[h] = m_new

    @pl.when(g == 0)
    def _new_tokens():
        col = lax.broadcasted_iota(I32, (rows, page), 1)
        tok = lax.broadcasted_iota(I32, (rows, page), 0)
        tok = jnp.where(tok >= t_new, tok - t_new, tok)
        keep = col <= tok
        pad = jnp.zeros((page - rows, dv), F32)
        for h in range(N_DIFF_HEADS):
            kn = jnp.concatenate([kn_ref[0, :, h * dv:(h + 1) * dv], pad], axis=0).astype(BF16)
            vn = jnp.concatenate([vn_ref[0, :, h * dv:(h + 1) * dv], pad], axis=0).astype(BF16)
            s = lax.dot_general(q_blockdiag(h), kn, _NT, preferred_element_type=F32)
            accumulate(h, jnp.where(keep, s, -jnp.inf), [vn], True)

    for h in range(N_DIFF_HEADS):
        qb = q_blockdiag(h)
        s = jnp.concatenate(
            [lax.dot_general(qb, kr[:, h * dv:(h + 1) * dv].astype(BF16), _NT, preferred_element_type=F32)
             for kr in k_refs], axis=1)
        accumulate(h, s, [vr[:, h * dv:(h + 1) * dv].astype(BF16) for vr in v_refs], False)

    @pl.when(g == pl.num_programs(1) - 1)
    def _finish():
        lam = _lambda(lq1_ref, lk1_ref, lq2_ref, lk2_ref, lam_init)
        for h in range(N_DIFF_HEADS):
            a = acc_ref[h] / l_ref[h]
            o = a - lam * pltpu.roll(a, t_new, 0)
            o = _rms(o, gsub_ref[...]) * (1.0 - lam_init)
            o_ref[0, :, h * dv:(h + 1) * dv] = jnp.where(row_q < t_new, o, 0.0)


def _diff_attention_decode(q, k_new, v_new, cache_k, cache_v, page_table, lams, g_sub, lam_init, t_new):
    batch, rows, width = q.shape
    assert rows == 2 * t_new == SUBLANE, "decode kernel stacks the two softmax maps of the new tokens in one sublane tile"
    dv = width // N_DIFF_HEADS
    dh = dv // 2
    n_pages = page_table.shape[1]
    page = cache_k.shape[1]
    n_pg = 8 if n_pages % 8 == 0 else 1
    steps = n_pages // n_pg
    seq_blk = lambda b, g, pt: (b, 0, 0)
    fixed = lambda b, g, pt: (0, 0)

    def page_spec(jj):
        return pl.BlockSpec((None, page, width), lambda b, g, pt: (pt[b * n_pages + g * n_pg + jj], 0, 0))

    grid_spec = pltpu.PrefetchScalarGridSpec(
        num_scalar_prefetch=1,
        grid=(batch, steps),
        in_specs=[pl.BlockSpec((1, rows, width), seq_blk)] * 3
                 + [pl.BlockSpec((1, dh), fixed)] * 4 + [pl.BlockSpec((1, dv), fixed)]
                 + [page_spec(jj) for jj in range(n_pg)] * 2,
        out_specs=pl.BlockSpec((1, rows, width), seq_blk),
        scratch_shapes=[pltpu.VMEM((N_DIFF_HEADS, rows, 1), F32), pltpu.VMEM((N_DIFF_HEADS, rows, 1), F32),
                        pltpu.VMEM((N_DIFF_HEADS, rows, dv), F32)],
    )
    return pl.pallas_call(
        functools.partial(_diff_decode_kernel, n_pg=n_pg, t_new=t_new, dh=dh, scale=dh ** -0.5, lam_init=lam_init),
        grid_spec=grid_spec,
        out_shape=jax.ShapeDtypeStruct((batch, rows, width), F32),
        compiler_params=_cparams(("parallel", "arbitrary"), 40),
        name="diff_attention_decode",
    )(page_table.reshape(-1), q, k_new, v_new, *lams, g_sub, *([cache_k] * n_pg), *([cache_v] * n_pg))


def _gla_chunk(q, k, v, g, s_t, chunk):
    nb = chunk // GLA_SUB
    row1 = lax.broadcasted_iota(I32, (chunk, 1), 0)
    b = g
    step = 1
    while step < chunk:
        b = b + jnp.where(row1 >= step, pltpu.roll(b, step, 0), 0.0)
        step *= 2
    b_last = b[chunk - 1:chunk, :]

    qe = (q * jnp.exp(b)).astype(BF16)
    o = lax.dot_general(qe, s_t.astype(BF16), _NT, preferred_element_type=F32)
    kd = (k * jnp.exp(b_last - b)).astype(BF16)
    s_new = s_t * jnp.exp(b_last) + lax.dot_general(v.astype(BF16), kd, _TN, preferred_element_type=F32)

    row = lax.broadcasted_iota(I32, (chunk, chunk), 0)
    col = lax.broadcasted_iota(I32, (chunk, chunk), 1)
    row_in_blk = row1 & (GLA_SUB - 1)

    att = jnp.where(col == row, jnp.sum(q * k, axis=-1, keepdims=True), 0.0)
    for d in range(1, min(GLA_SUB, chunk)):
        k_sh = pltpu.roll(k, d, 0)
        b_sh = pltpu.roll(b, d, 0)
        decay = jnp.exp(jnp.where(row_in_blk >= d, b - b_sh, -jnp.inf))
        w = jnp.sum(q * k_sh * decay, axis=-1, keepdims=True)
        att = att + jnp.where(col == row - d, w, 0.0)

    dk = q.shape[1]
    for j in range(nb - 1):
        r0 = (j + 1) * GLA_SUB
        b_end = b[r0 - 1:r0, :]
        kj = k[r0 - GLA_SUB:r0] * jnp.exp(b_end - b[r0 - GLA_SUB:r0])
        above = [jnp.zeros((r0 - GLA_SUB, dk), F32)] if j else []
        kj = jnp.concatenate(above + [kj, jnp.zeros((chunk - r0, dk), F32)], axis=0)
        qj = (q[r0:] * jnp.exp(b[r0:] - b_end)).astype(BF16)
        a = lax.dot_general(qj, kj.astype(BF16), _NT, preferred_element_type=F32)
        att = att + jnp.concatenate([jnp.zeros((r0, chunk), F32), a], axis=0)

    o = o + jnp.dot(att.astype(BF16), v.astype(BF16), preferred_element_type=F32)
    return o, s_new


def _gla_kernel(gqk_ref, gv_ref, gr_ref, glog_ref, s0_ref, ggla_ref, o_ref, s_ref, st_ref,
                *, chunk, t_valid, dk, dv, scale):
    tb = pl.program_id(1)
    t_blk = gqk_ref.shape[1]
    n_heads = N_GLA_HEADS

    @pl.when(tb == 0)
    def _load_state():
        for h in range(n_heads):
            st_ref[h] = s0_ref[0, h].T

    g_gla = ggla_ref[...]

    def body(c, carry):
        r0 = pl.multiple_of(c * chunk, chunk)
        rows = pl.ds(r0, chunk)
        pos = tb * t_blk + r0 + lax.broadcasted_iota(I32, (chunk, 1), 0)
        live = pos < t_valid
        for h in range(n_heads):
            q = gqk_ref[0, rows, h * dk:(h + 1) * dk] * scale
            k = gqk_ref[0, rows, (n_heads + h) * dk:(n_heads + h + 1) * dk]
            v = gv_ref[0, rows, h * dv:(h + 1) * dv]
            g = jnp.where(live, glog_ref[0, rows, h * dk:(h + 1) * dk], 0.0)
            k = jnp.where(live, k, 0.0)
            o, s_new = _gla_chunk(q, k, v, g, st_ref[h], chunk)
            st_ref[h] = s_new
            gr = gr_ref[0, rows, h * dv:(h + 1) * dv]
            o_ref[0, rows, h * dv:(h + 1) * dv] = (_rms(o, g_gla) * (gr * _sigmoid(gr))).astype(o_ref.dtype)
        return carry

    lax.fori_loop(0, t_blk // chunk, body, 0)

    @pl.when(tb == pl.num_programs(1) - 1)
    def _store_state():
        for h in range(n_heads):
            s_ref[0, h] = st_ref[h].T


def _gla(gqk, gv, gr, glog, s0, g_gla, t_valid, out_dtype):
    batch, seq, _ = gqk.shape
    dk, dv = s0.shape[2], s0.shape[3]
    chunk = next(c for c in (64, 32, 16, 8) if seq % c == 0)
    t_blk = _tile(seq, 256)
    tok = lambda b, t: (b, t, 0)
    return pl.pallas_call(
        functools.partial(_gla_kernel, chunk=chunk, t_valid=t_valid, dk=dk, dv=dv, scale=dk ** -0.5),
        grid=(batch, seq // t_blk),
        in_specs=[
            pl.BlockSpec((1, t_blk, gqk.shape[2]), tok),
            pl.BlockSpec((1, t_blk, gv.shape[2]), tok),
            pl.BlockSpec((1, t_blk, gr.shape[2]), tok),
            pl.BlockSpec((1, t_blk, glog.shape[2]), tok),
            pl.BlockSpec((1, N_GLA_HEADS, dk, dv), lambda b, t: (b, 0, 0, 0)),
            pl.BlockSpec((1, dv), lambda b, t: (0, 0)),
        ],
        out_specs=[
            pl.BlockSpec((1, t_blk, gv.shape[2]), tok),
            pl.BlockSpec((1, N_GLA_HEADS, dk, dv), lambda b, t: (b, 0, 0, 0)),
        ],
        out_shape=[jax.ShapeDtypeStruct(gv.shape, out_dtype), jax.ShapeDtypeStruct(s0.shape, F32)],
        scratch_shapes=[pltpu.VMEM((N_GLA_HEADS, dv, dk), F32)],
        compiler_params=_cparams(("parallel", "arbitrary"), 32),
        name="gla",
    )(gqk, gv, gr, glog, s0, g_gla)


def _cross_kernel(q_ref, mk_ref, mv_ref, o_ref, *, dh, scale):
    for h in range(N_MEM_HEADS):
        sl = slice(h * dh, (h + 1) * dh)
        q = (q_ref[0, :, sl].astype(F32) * scale).astype(BF16)
        s = lax.dot_general(q, mk_ref[0, :, sl].astype(BF16), _NT, preferred_element_type=F32)
        pr = jnp.exp(s - jnp.max(s, axis=-1, keepdims=True))
        pr = pr / jnp.sum(pr, axis=-1, keepdims=True)
        o_ref[0, :, sl] = jnp.dot(pr.astype(BF16), mv_ref[0, :, sl].astype(BF16),
                                  preferred_element_type=F32).astype(o_ref.dtype)


def _cross_attention(q, mk, mv):
    batch, seq, d = q.shape
    mem = mk.shape[1]
    dh = d // N_MEM_HEADS
    tq = _tile(seq, 512)
    return pl.pallas_call(
        functools.partial(_cross_kernel, dh=dh, scale=dh ** -0.5),
        grid=(batch, seq // tq),
        in_specs=[
            pl.BlockSpec((1, tq, d), lambda b, t: (b, t, 0)),
            pl.BlockSpec((1, mem, d), lambda b, t: (b, 0, 0)),
            pl.BlockSpec((1, mem, d), lambda b, t: (b, 0, 0)),
        ],
        out_specs=pl.BlockSpec((1, tq, d), lambda b, t: (b, t, 0)),
        out_shape=jax.ShapeDtypeStruct(q.shape, BF16 if tq % 16 == 0 else F32),
        compiler_params=_cparams(("parallel", "arbitrary"), 40),
        name="cross_attention",
    )(q, mk, mv)


def _route_kernel(x_ref, g_ref, wh_ref, wl_ref, xn_ref, idx_ref, wgt_ref):
    xn = _rms(x_ref[...], g_ref[...])
    xn_ref[...] = xn
    xh = xn.astype(BF16)
    xl = (xn - xh.astype(F32)).astype(BF16)
    logits = (jnp.dot(xh, wh_ref[...], preferred_element_type=F32)
              + jnp.dot(xl, wh_ref[...], preferred_element_type=F32)
              + jnp.dot(xh, wl_ref[...], preferred_element_type=F32))
    lane = lax.broadcasted_iota(I32, logits.shape, 1)
    lane_f = lane.astype(F32)
    neg = -jnp.inf

    def first_max(vals):
        top = jnp.max(vals, axis=-1, keepdims=True)
        first = jnp.min(jnp.where(vals == top, lane_f, float(LANE)), axis=-1, keepdims=True)
        return top, first.astype(I32)

    is_grp = lane < N_GROUPS
    g_top, g_idx = first_max(jnp.where(is_grp, logits, neg))
    g_w = 1.0 / jnp.sum(jnp.where(is_grp, jnp.exp(logits - g_top), 0.0), axis=-1, keepdims=True)
    lo = N_GROUPS + g_idx * EXPERTS_PER_GROUP
    le = jnp.where((lane >= lo) & (lane < lo + EXPERTS_PER_GROUP), logits, neg)
    v1, i1 = first_max(le)
    v2, i2 = first_max(jnp.where(lane == i1, neg, le))
    e2 = jnp.exp(v2 - v1)
    w1 = g_w / (1.0 + e2)
    w2 = g_w * e2 / (1.0 + e2)
    idx_ref[...] = jnp.where(lane == 0, i1 - N_GROUPS, jnp.where(lane == 1, i2 - N_GROUPS, 0))
    wgt_ref[...] = jnp.where(lane == 0, w1, jnp.where(lane == 1, w2, 0.0))


def _route(x, g_ffn, w_hi, w_lo):
    n, d = x.shape
    tm = _tile(n, 512)
    row = lambda i: (i, 0)
    fixed = lambda i: (0, 0)
    return pl.pallas_call(
        _route_kernel,
        grid=(n // tm,),
        in_specs=[pl.BlockSpec((tm, d), row), pl.BlockSpec((1, d), fixed),
                  pl.BlockSpec((d, LANE), fixed), pl.BlockSpec((d, LANE), fixed)],
        out_specs=[pl.BlockSpec((tm, d), row), pl.BlockSpec((tm, LANE), row), pl.BlockSpec((tm, LANE), row)],
        out_shape=[jax.ShapeDtypeStruct((n, d), F32), jax.ShapeDtypeStruct((n, LANE), I32),
                   jax.ShapeDtypeStruct((n, LANE), F32)],
        compiler_params=_cparams(("parallel",), 40),
        name="moe_route",
    )(x, g_ffn, w_hi, w_lo)


def _dispatch_kernel(pos_ref, x_ref, xs_in_ref, xs_ref, sem):
    del xs_in_ref
    i = pl.program_id(0)
    tm = x_ref.shape[0]

    def row_copy(r, kk):
        dst = pos_ref[(i * tm + r) * 2 + kk]
        return pltpu.make_async_copy(x_ref.at[pl.ds(r, 1), :], xs_ref.at[pl.ds(dst, 1), :], sem)

    def start(r, c):
        row_copy(r, 0).start()
        row_copy(r, 1).start()
        return c

    def wait(r, c):
        row_copy(r, 0).wait()
        row_copy(r, 1).wait()
        return c

    lax.fori_loop(0, tm, start, 0)
    lax.fori_loop(0, tm, wait, 0)


def _dispatch(xn, pos, n_slots):
    n, d = xn.shape
    tm = _tile(n, 256)
    grid_spec = pltpu.PrefetchScalarGridSpec(
        num_scalar_prefetch=1,
        grid=(n // tm,),
        in_specs=[pl.BlockSpec((tm, d), lambda i, pos: (i, 0)), pl.BlockSpec(memory_space=pl.ANY)],
        out_specs=pl.BlockSpec(memory_space=pl.ANY),
        scratch_shapes=[pltpu.SemaphoreType.DMA(())],
    )
    return pl.pallas_call(
        _dispatch_kernel,
        grid_spec=grid_spec,
        out_shape=jax.ShapeDtypeStruct((n_slots, d), F32),
        input_output_aliases={2: 0},
        compiler_params=_cparams(("arbitrary",), 32),
        name="moe_dispatch",
    )(pos, xn, jnp.zeros((n_slots, d), F32))


def _experts_kernel(te_ref, nu_ref, x_ref, wg_ref, wu_ref, wd_ref, y_ref):
    t = pl.program_id(0)

    @pl.when(t < nu_ref[0])
    def _mlp():
        x = x_ref[...].astype(BF16)
        hg = jnp.dot(x, wg_ref[...], preferred_element_type=F32)
        hu = jnp.dot(x, wu_ref[...], preferred_element_type=F32)
        h = (hg * _sigmoid(hg)) * hu
        y_ref[...] = jnp.dot(h.astype(BF16), wd_ref[...], preferred_element_type=F32)

    @pl.when(t >= nu_ref[0])
    def _unused():
        y_ref[...] = jnp.zeros(y_ref.shape, F32)


def _experts(xs, tile_expert, n_used, w_gate, w_up, w_down, tm):
    n_slots, d = xs.shape
    de = w_gate.shape[2]
    grid_spec = pltpu.PrefetchScalarGridSpec(
        num_scalar_prefetch=2,
        grid=(n_slots // tm,),
        in_specs=[
            pl.BlockSpec((tm, d), lambda t, te, nu: (t, 0)),
            pl.BlockSpec((None, d, de), lambda t, te, nu: (te[t], 0, 0)),
            pl.BlockSpec((None, d, de), lambda t, te, nu: (te[t], 0, 0)),
            pl.BlockSpec((None, de, d), lambda t, te, nu: (te[t], 0, 0)),
        ],
        out_specs=pl.BlockSpec((tm, d), lambda t, te, nu: (t, 0)),
    )
    return pl.pallas_call(
        _experts_kernel,
        grid_spec=grid_spec,
        out_shape=jax.ShapeDtypeStruct((n_slots, d), F32),
        compiler_params=_cparams(("arbitrary",), 40),
        name="moe_experts",
    )(tile_expert, n_used, xs, w_gate, w_up, w_down)


def _combine_kernel(pos_ref, x_ref, wgt_ref, g_ref, ys_ref, o_ref, buf_ref, sem):
    i = pl.program_id(0)
    tm = x_ref.shape[0]

    def row_copy(r, kk):
        src = pos_ref[(i * tm + r) * 2 + kk]
        return pltpu.make_async_copy(ys_ref.at[pl.ds(src, 1), :], buf_ref.at[kk, pl.ds(r, 1), :], sem)

    def start(r, c):
        row_copy(r, 0).start()
        row_copy(r, 1).start()
        return c

    def wait(r, c):
        row_copy(r, 0).wait()
        row_copy(r, 1).wait()
        return c

    lax.fori_loop(0, tm, start, 0)
    lax.fori_loop(0, tm, wait, 0)
    wgt = wgt_ref[...]
    y = x_ref[...] + (wgt[:, 0:1] * buf_ref[0] + wgt[:, 1:2] * buf_ref[1])
    o_ref[...] = _rms(y, g_ref[...])


def _combine(x, wgt, g_final, ys, pos):
    n, d = x.shape
    tm = _tile(n, 256)
    grid_spec = pltpu.PrefetchScalarGridSpec(
        num_scalar_prefetch=1,
        grid=(n // tm,),
        in_specs=[pl.BlockSpec((tm, d), lambda i, pos: (i, 0)), pl.BlockSpec((tm, LANE), lambda i, pos: (i, 0)),
                  pl.BlockSpec((1, d), lambda i, pos: (0, 0)), pl.BlockSpec(memory_space=pl.ANY)],
        out_specs=pl.BlockSpec((tm, d), lambda i, pos: (i, 0)),
        scratch_shapes=[pltpu.VMEM((2, tm, d), F32), pltpu.SemaphoreType.DMA(())],
    )
    return pl.pallas_call(
        _combine_kernel,
        grid_spec=grid_spec,
        out_shape=jax.ShapeDtypeStruct((n, d), F32),
        compiler_params=_cparams(("arbitrary",), 32),
        name="moe_combine",
    )(pos, x, wgt, g_final, ys)


def _moe_and_final_norm(x, g_ffn, w_route_hi, w_route_lo, w_gate, w_up, w_down, g_final):
    n, d = x.shape
    xn, ridx, rw = _route(x, g_ffn, w_route_hi, w_route_lo)
    tm = 256 if n >= 2048 else 128
    e = ridx[:, :2].reshape(-1)
    onehot = (e[:, None] == jnp.arange(N_EXPERTS, dtype=I32)[None, :]).astype(I32)
    csum = jnp.cumsum(onehot, axis=0)
    counts = csum[-1]
    padded = (counts + tm - 1) // tm * tm
    p_end = jnp.cumsum(padded)
    pos = jnp.sum(onehot * ((p_end - padded)[None, :] + csum - 1), axis=1).astype(I32)
    n_tiles = (2 * n + N_EXPERTS * (tm - 1)) // tm + 1
    tile_start = jnp.arange(n_tiles, dtype=I32) * tm
    tile_expert = jnp.minimum(jnp.sum((tile_start[:, None] >= p_end[None, :]).astype(I32), axis=1), N_EXPERTS - 1)
    n_used = (p_end[-1:] // tm).astype(I32)
    xs = _dispatch(xn, pos, n_tiles * tm)
    ys = _experts(xs, tile_expert.astype(I32), n_used, w_gate, w_up, w_down, tm)
    return _combine(x, rw, g_final, ys, pos)


def _layer(x3, s0, t_valid, mk, mv, diff_fn, lw):
    batch, seq, d = x3.shape
    n = batch * seq
    x = x3.reshape(n, d)
    q, k, v, gqk, gv, gr, glog = _in_projection(x, lw["g_mix"], lw["w_in_main"], lw["w_in_ga"], lw["w_a2p"], lw["b_a"])
    d_out = diff_fn(q, k, v)
    r3 = lambda a: a.reshape(batch, seq, a.shape[-1])
    g_out, s_new = _gla(r3(gqk), r3(gv), r3(gr), r3(glog), s0, lw["g_gla"], t_valid,
                        BF16 if seq % 16 == 0 else F32)
    x1 = _matmul_residual([d_out.reshape(n, -1), g_out.reshape(n, -1)], lw["w_o"], x, "mixer_out")
    xq = _norm_matmul(x1, lw["g_cross"], lw["w_q_mem"], BF16 if seq % 16 == 0 else F32, "cross_q")
    ca = _cross_attention(xq.reshape(batch, seq, d), mk, mv)
    x2 = _matmul_residual([ca.reshape(n, d)], lw["w_o_mem"], x1, "cross_out")
    return x2, k, v, s_new


def kernel(x_prompt, x_sample, mem_prompt, cache_k, cache_v, state_gla, cache_mem_k, cache_mem_v, page_table, g_mix, w_in, w_a2, b_a, lambda_q1, lambda_k1, lambda_q2, lambda_k2, g_subln, g_gla, w_o, g_mem, g_cross, w_q_mem, w_k_mem, w_v_mem, w_o_mem, g_ffn, w_group, w_router, w_gate, w_up, w_down, g_final):
    depth = w_in.shape[0]
    batch, seq, d = x_prompt.shape
    dec_batch, dec_seq, _ = x_sample.shape
    mem_len = mem_prompt.shape[1]
    width = d // 2
    main_cols = w_in.shape[2] - GATE_RANK
    assert main_cols == 6 * width and w_a2.shape[1:] == (GATE_RANK, width // 2)
    dk_gla = width // N_GLA_HEADS // 2
    dv_gla = width // N_GLA_HEADS
    dec_pad = -(-dec_seq // SUBLANE) * SUBLANE

    xp = x_prompt
    xs = jnp.pad(x_sample, ((0, 0), (0, dec_pad - dec_seq), (0, 0)))
    s0_prompt = jnp.zeros((batch, N_GLA_HEADS, dk_gla, dv_gla), F32)
    row2 = lambda a: a.reshape(1, -1)
    outs = {name: [] for name in ("kp", "vp", "sp", "mkp", "mvp", "ks", "vs", "ss")}
    for l in range(depth):
        lam_init = 0.8 - 0.6 * math.exp(-0.3 * l)
        lw = dict(
            g_mix=row2(g_mix[l]), g_gla=row2(g_gla[l]), g_cross=row2(g_cross[l]), b_a=row2(b_a[l]),
            w_in_main=w_in[l][:, :main_cols].astype(BF16),
            w_in_ga=jnp.pad(w_in[l][:, main_cols:], ((0, 0), (0, LANE - GATE_RANK))).astype(BF16),
            w_a2p=jnp.pad(w_a2[l], ((0, LANE - GATE_RANK), (0, 0))).astype(BF16),
            w_o=w_o[l].astype(BF16), w_q_mem=w_q_mem[l].astype(BF16), w_o_mem=w_o_mem[l].astype(BF16),
        )
        lams = (row2(lambda_q1[l]), row2(lambda_k1[l]), row2(lambda_q2[l]), row2(lambda_k2[l]))
        g_sub = row2(g_subln[l])
        w_route = jnp.pad(jnp.concatenate([w_group[l], w_router[l]], axis=1),
                          ((0, 0), (0, LANE - N_GROUPS - N_EXPERTS)))
        w_route_hi = w_route.astype(BF16)
        w_route_lo = (w_route - w_route_hi.astype(F32)).astype(BF16)
        wg, wu, wd = w_gate[l].astype(BF16), w_up[l].astype(BF16), w_down[l].astype(BF16)

        mem2 = mem_prompt.reshape(batch * mem_len, d)
        mkp = _norm_matmul(mem2, row2(g_mem[l]), w_k_mem[l].astype(BF16), F32, "mem_k")
        mvp = _norm_matmul(mem2, row2(g_mem[l]), w_v_mem[l].astype(BF16), F32, "mem_v")

        prompt_attn = lambda q, k, v: _diff_attention_prompt(q, k, v, lams, g_sub, lam_init, batch, seq, BF16)
        xp2, kp, vp, sp = _layer(xp, s0_prompt, seq, mkp.reshape(batch, mem_len, d), mvp.reshape(batch, mem_len, d),
                                 prompt_attn, lw)

        ck = cache_k[l].reshape(cache_k.shape[1], cache_k.shape[2], width)
        cv = cache_v[l].reshape(cache_v.shape[1], cache_v.shape[2], width)
        r3 = lambda a: a.reshape(dec_batch, dec_pad, width)
        sample_attn = lambda q, k, v: _diff_attention_decode(r3(q), r3(k), r3(v), ck, cv, page_table, lams, g_sub,
                                                             lam_init, dec_seq)
        xs2, ks, vs, ss = _layer(xs, state_gla[l], dec_seq, cache_mem_k[l].reshape(dec_batch, mem_len, d),
                                 cache_mem_v[l].reshape(dec_batch, mem_len, d), sample_attn, lw)

        last = l == depth - 1
        assert last, "the MoE kernel fuses the final norm; deeper stacks need an un-normed variant"
        moe = functools.partial(_moe_and_final_norm, g_ffn=row2(g_ffn[l]), w_route_hi=w_route_hi,
                                w_route_lo=w_route_lo, w_gate=wg, w_up=wu, w_down=wd, g_final=row2(g_final))
        xp = moe(xp2).reshape(batch, seq, d)
        xs = moe(xs2).reshape(dec_batch, dec_pad, d)

        outs["kp"].append(kp.reshape(batch, seq, N_DIFF_HEADS, -1))
        outs["vp"].append(vp.reshape(batch, seq, N_DIFF_HEADS, -1))
        outs["sp"].append(sp)
        outs["mkp"].append(mkp.reshape(batch, mem_len, N_MEM_HEADS, -1))
        outs["mvp"].append(mvp.reshape(batch, mem_len, N_MEM_HEADS, -1))
        outs["ks"].append(ks.reshape(dec_batch, dec_pad, N_DIFF_HEADS, -1)[:, :dec_seq])
        outs["vs"].append(vs.reshape(dec_batch, dec_pad, N_DIFF_HEADS, -1)[:, :dec_seq])
        outs["ss"].append(ss)

    st = lambda name: jnp.stack(outs[name])
    return (xp, xs[:, :dec_seq], st("kp"), st("vp"), st("sp"), st("mkp"), st("mvp"), st("ks"), st("vs"), st("ss"))
```

```python
import functools
import math

import jax
import jax.numpy as jnp
from jax import lax
from jax.experimental import pallas as pl
from jax.experimental.pallas import tpu as pltpu

F32 = jnp.float32
BF16 = jnp.bfloat16
I32 = jnp.int32

N_DIFF_HEADS = 4
N_GLA_HEADS = 4
GATE_RANK = 16
GATE_TAU = 16.0
N_MEM_HEADS = 4
N_GROUPS = 4
EXPERTS_PER_GROUP = 8
N_EXPERTS = N_GROUPS * EXPERTS_PER_GROUP
RMS_EPS = 1e-6
GLA_SUB = 8
LOG2E = math.log2(math.e)

LANE = 128
SUBLANE = 8
MIB = 1024 * 1024
DMA_UNROLL = 8

_NT = (((1,), (1,)), ((), ()))
_TN = (((0,), (0,)), ((), ()))


def _cparams(semantics, vmem_mib):
    return pltpu.CompilerParams(dimension_semantics=semantics, vmem_limit_bytes=int(vmem_mib * MIB))


def _tile(n, target):
    if n <= target:
        return n
    t = target - target % SUBLANE
    while t >= SUBLANE:
        if n % t == 0:
            return t
        t -= SUBLANE
    raise ValueError(f"no sublane-aligned tile of {n} below {target}")


def _rms(x, g):
    ms = jnp.mean(x * x, axis=-1, keepdims=True)
    return x * lax.rsqrt(ms + RMS_EPS) * g


def _sigmoid(x):
    return 1.0 / (1.0 + jnp.exp(-x))


def _store_heads(o_ref, val, n_heads):
    w = val.shape[1] // n_heads
    for h in range(n_heads):
        o_ref[:, h, :] = val[:, h * w:(h + 1) * w].astype(o_ref.dtype)


def _inproj_kernel(x_ref, g_ref, w_ref, wga_ref, wa2_ref, ba_ref,
                   qs_ref, k_ref, v_ref, kb_ref, vb_ref, gqk_ref, gv_ref, gr_ref, glog_ref, xn_ref, *, q_scale):
    j = pl.program_id(1)

    @pl.when(j == 0)
    def _norm():
        xn_ref[...] = _rms(x_ref[...], g_ref[...]).astype(BF16)

    def proj():
        return jnp.dot(xn_ref[...], w_ref[...], preferred_element_type=F32)

    @pl.when(j == 0)
    def _diff_q():
        qs_ref[...] = (proj() * q_scale).astype(BF16)

    for idx, (o_ref, ob_ref) in enumerate(((k_ref, kb_ref), (v_ref, vb_ref))):
        @pl.when(j == 1 + idx)
        def _diff_kv(o_ref=o_ref, ob_ref=ob_ref):
            r = proj()
            _store_heads(o_ref, r, N_DIFF_HEADS)
            ob_ref[...] = r.astype(BF16)

    for idx, o_ref in enumerate((gqk_ref, gv_ref, gr_ref)):
        @pl.when(j == 3 + idx)
        def _gla_part(o_ref=o_ref):
            o_ref[...] = proj()

    @pl.when(j == 6)
    def _gate():
        ga = jnp.dot(xn_ref[...], wga_ref[...], preferred_element_type=F32)
        z = jnp.dot(ga.astype(BF16), wa2_ref[...], preferred_element_type=F32) + ba_ref[...]
        glog_ref[...] = (jnp.minimum(z, 0.0) - jnp.log1p(jnp.exp(-jnp.abs(z)))) * (1.0 / GATE_TAU)


def _in_projection(x, g_mix, w_main, w_ga, w_a2p, b_a):
    n, d = x.shape
    wblk = d // 2
    assert w_main.shape[1] == 6 * wblk
    dv = wblk // N_DIFF_HEADS
    gk = w_a2p.shape[1]
    tm = _tile(n, 512)
    row = lambda i, j: (i, 0)
    row3 = lambda i, j: (i, 0, 0)
    fixed = lambda i, j: (0, 0)
    flat = lambda dt: (jax.ShapeDtypeStruct((n, wblk), dt), pl.BlockSpec((tm, wblk), row))
    heads = (jax.ShapeDtypeStruct((n, N_DIFF_HEADS, dv), F32), pl.BlockSpec((tm, N_DIFF_HEADS, dv), row3))
    outs = [flat(BF16), heads, heads, flat(BF16), flat(BF16), flat(F32), flat(F32), flat(F32),
            (jax.ShapeDtypeStruct((n, gk), F32), pl.BlockSpec((tm, gk), row))]
    return pl.pallas_call(
        functools.partial(_inproj_kernel, q_scale=(dv // 2) ** -0.5 * LOG2E),
        grid=(n // tm, 7),
        in_specs=[
            pl.BlockSpec((tm, d), row),
            pl.BlockSpec((1, d), fixed),
            pl.BlockSpec((d, wblk), lambda i, j: (0, jnp.minimum(j, 5))),
            pl.BlockSpec((d, LANE), fixed),
            pl.BlockSpec((LANE, gk), fixed),
            pl.BlockSpec((1, gk), fixed),
        ],
        out_specs=[o[1] for o in outs],
        out_shape=[o[0] for o in outs],
        scratch_shapes=[pltpu.VMEM((tm, d), BF16)],
        compiler_params=_cparams(("parallel", "arbitrary"), 56),
        name="in_projection",
    )(x, g_mix, w_main, w_ga, w_a2p, b_a)


def _norm_mm_kernel(x_ref, g_ref, w_ref, o_ref, xn_ref, *, out_scale, n_heads):
    @pl.when(pl.program_id(1) == 0)
    def _norm():
        xn_ref[...] = _rms(x_ref[...], g_ref[...]).astype(BF16)

    r = jnp.dot(xn_ref[...], w_ref[...], preferred_element_type=F32)
    if out_scale != 1.0:
        r = r * out_scale
    if n_heads:
        _store_heads(o_ref, r, n_heads)
    else:
        o_ref[...] = r.astype(o_ref.dtype)


def _norm_matmul(x, g, w, out_dtype, name, out_scale=1.0, n_heads=0):
    n, d = x.shape
    dout = w.shape[1]
    tm = _tile(n, 512)
    tn = dout if n_heads else _tile(dout, 1024)
    if n_heads:
        out_spec = pl.BlockSpec((tm, n_heads, dout // n_heads), lambda i, j: (i, 0, 0))
        out_shape = jax.ShapeDtypeStruct((n, n_heads, dout // n_heads), out_dtype)
    else:
        out_spec = pl.BlockSpec((tm, tn), lambda i, j: (i, j))
        out_shape = jax.ShapeDtypeStruct((n, dout), out_dtype)
    return pl.pallas_call(
        functools.partial(_norm_mm_kernel, out_scale=out_scale, n_heads=n_heads),
        grid=(n // tm, dout // tn),
        in_specs=[
            pl.BlockSpec((tm, d), lambda i, j: (i, 0)),
            pl.BlockSpec((1, d), lambda i, j: (0, 0)),
            pl.BlockSpec((d, tn), lambda i, j: (0, j)),
        ],
        out_specs=out_spec,
        out_shape=out_shape,
        scratch_shapes=[pltpu.VMEM((tm, d), BF16)],
        compiler_params=_cparams(("parallel", "arbitrary"), 48),
        name=name,
    )(x, g, w)


def _mm_res_kernel(*refs, widths):
    a_refs = refs[:len(widths)]
    w_ref, res_ref, o_ref = refs[len(widths):]
    acc = res_ref[...]
    k0 = 0
    for a_ref, kw in zip(a_refs, widths):
        acc = acc + jnp.dot(a_ref[...].astype(BF16), w_ref[k0:k0 + kw, :], preferred_element_type=F32)
        k0 += kw
    o_ref[...] = acc


def _matmul_residual(pieces, w, res, name):
    n, dout = res.shape
    widths = tuple(p.shape[1] for p in pieces)
    assert sum(widths) == w.shape[0]
    tm = _tile(n, 512)
    tn = _tile(dout, 1024)
    in_specs = [pl.BlockSpec((tm, kw), lambda i, j: (i, 0)) for kw in widths]
    in_specs += [pl.BlockSpec((w.shape[0], tn), lambda i, j: (0, j)),
                 pl.BlockSpec((tm, tn), lambda i, j: (i, j))]
    return pl.pallas_call(
        functools.partial(_mm_res_kernel, widths=widths),
        grid=(n // tm, dout // tn),
        in_specs=in_specs,
        out_specs=pl.BlockSpec((tm, tn), lambda i, j: (i, j)),
        out_shape=jax.ShapeDtypeStruct((n, dout), F32),
        compiler_params=_cparams(("parallel", "arbitrary"), 40),
        name=name,
    )(*pieces, w, res)


def _lambda(lq1_ref, lk1_ref, lq2_ref, lk2_ref, lam_init):
    d1 = jnp.sum(lq1_ref[...] * lk1_ref[...], axis=-1, keepdims=True)
    d2 = jnp.sum(lq2_ref[...] * lk2_ref[...], axis=-1, keepdims=True)
    return jnp.exp(d1) - jnp.exp(d2) + lam_init


def _diff_prompt_kernel(qi_ref, ki_ref, q_ref, k_ref, v_ref, lq1_ref, lk1_ref, lq2_ref, lk2_ref, gsub_ref,
                        o_ref, m_ref, l_ref, acc_ref, *, dh, sub, lam_init):
    p = pl.program_id(2)
    qi = qi_ref[p]
    ki = ki_ref[p]
    tq, tk = q_ref.shape[0], k_ref.shape[0]
    dv = 2 * dh

    @pl.when(ki == 0)
    def _init():
        m_ref[...] = jnp.full(m_ref.shape, -jnp.inf, F32)
        l_ref[...] = jnp.zeros(l_ref.shape, F32)
        acc_ref[...] = jnp.zeros(acc_ref.shape, F32)

    def update(r, diagonal):
        rows = slice(r * sub, (r + 1) * sub)
        ncol = (r + 1) * sub if diagonal else tk
        v = v_ref[0:ncol, :]
        if diagonal:
            keep = lax.broadcasted_iota(I32, (sub, sub), 1) <= lax.broadcasted_iota(I32, (sub, sub), 0)
        for mi in range(2):
            s = lax.dot_general(q_ref[rows, mi * dh:(mi + 1) * dh], k_ref[0:ncol, mi * dh:(mi + 1) * dh], _NT,
                                preferred_element_type=F32)
            blocks = [s[:, c * LANE:(c + 1) * LANE] for c in range(ncol // LANE)]
            if diagonal:
                first = ncol // LANE - sub // LANE
                for c in range(sub // LANE):
                    blocks[first + c] = jnp.where(keep[:, c * LANE:(c + 1) * LANE], blocks[first + c], -jnp.inf)
            m_cur = blocks[0]
            for blk in blocks[1:]:
                m_cur = jnp.maximum(m_cur, blk)
            m_prev = m_ref[mi, rows, :]
            m_new = jnp.maximum(m_prev, jnp.max(m_cur, axis=-1, keepdims=True))
            alpha = jnp.exp2(m_prev - m_new)
            l_add = None
            pr = []
            for blk in blocks:
                e = jnp.exp2(blk - m_new)
                l_add = e if l_add is None else l_add + e
                pr.append(e.astype(BF16))
            pv = jnp.dot(jnp.concatenate(pr, axis=1), v, preferred_element_type=F32)
            l_ref[mi, rows, :] = alpha * l_ref[mi, rows, :] + l_add
            acc_ref[mi, rows, :] = jnp.concatenate([alpha] * (dv // LANE), axis=1) * acc_ref[mi, rows, :] + pv
            m_ref[mi, rows, :] = m_new

    @pl.when(ki < qi)
    def _below_diagonal():
        for r in range(tq // sub):
            update(r, False)

    @pl.when(ki == qi)
    def _diagonal():
        lam = _lambda(lq1_ref, lk1_ref, lq2_ref, lk2_ref, lam_init)
        for r in range(tq // sub):
            update(r, True)
            rows = slice(r * sub, (r + 1) * sub)
            l1 = jnp.sum(l_ref[0, rows, :], axis=-1, keepdims=True)
            l2 = jnp.sum(l_ref[1, rows, :], axis=-1, keepdims=True)
            o = acc_ref[0, rows, :] / l1 - lam * (acc_ref[1, rows, :] / l2)
            o_ref[rows, :] = (_rms(o, gsub_ref[...]) * (1.0 - lam_init)).astype(o_ref.dtype)


def _diff_attention_prompt(q, k, v, lams, g_sub, lam_init, batch, seq, out_dtype):
    n, width = q.shape
    dv = width // N_DIFF_HEADS
    dh = dv // 2
    tq = _tile(seq, 512)
    sub = _tile(tq, 128)
    assert sub % LANE == 0 or tq == sub
    nq = seq // tq
    pairs = [(a, b) for a in range(nq) for b in range(a + 1)]
    qi = jnp.asarray([a for a, _ in pairs], I32)
    ki = jnp.asarray([b for _, b in pairs], I32)
    grid_spec = pltpu.PrefetchScalarGridSpec(
        num_scalar_prefetch=2,
        grid=(batch, N_DIFF_HEADS, len(pairs)),
        in_specs=[
            pl.BlockSpec((tq, dv), lambda b, h, p, qi, ki: (b * nq + qi[p], h)),
            pl.BlockSpec((tq, dv), lambda b, h, p, qi, ki: (b * nq + ki[p], h)),
            pl.BlockSpec((tq, dv), lambda b, h, p, qi, ki: (b * nq + ki[p], h)),
        ] + [pl.BlockSpec((1, dh), lambda b, h, p, qi, ki: (0, 0))] * 4
          + [pl.BlockSpec((1, dv), lambda b, h, p, qi, ki: (0, 0))],
        out_specs=pl.BlockSpec((tq, dv), lambda b, h, p, qi, ki: (b * nq + qi[p], h)),
        scratch_shapes=[pltpu.VMEM((2, tq, LANE), F32), pltpu.VMEM((2, tq, LANE), F32), pltpu.VMEM((2, tq, dv), F32)],
    )
    return pl.pallas_call(
        functools.partial(_diff_prompt_kernel, dh=dh, sub=sub, lam_init=lam_init),
        grid_spec=grid_spec,
        out_shape=jax.ShapeDtypeStruct((n, width), out_dtype),
        compiler_params=_cparams(("parallel", "parallel", "arbitrary"), 40),
        name="diff_attention_prompt",
    )(qi, ki, q, k, v, *lams, g_sub)


def _diff_decode_kernel(pt_ref, q_ref, kn_ref, vn_ref, lq1_ref, lk1_ref, lq2_ref, lk2_ref, gsub_ref, *rest,
                        n_pg, t_new, dh, lam_init):
    k_refs = rest[:n_pg]
    v_refs = rest[n_pg:2 * n_pg]
    o_ref, m_ref, l_ref, acc_ref, qall_ref = rest[2 * n_pg:]
    g = pl.program_id(1)
    n_heads = N_DIFF_HEADS
    dv = 2 * dh
    rows = 2 * t_new
    page = k_refs[0].shape[0]
    kv_rows = page * n_heads
    row_q = lax.broadcasted_iota(I32, (rows, dv), 0)
    lane_q = lax.broadcasted_iota(I32, (rows, dv), 1)

    own_head = ((lax.broadcasted_iota(I32, (n_heads * rows, LANE), 1) & (n_heads - 1))
                == lax.broadcasted_iota(I32, (n_heads * rows, LANE), 0) >> (rows.bit_length() - 1))

    def scores(k2, keep):
        s = lax.dot_general(qall_ref[...], k2, _NT, preferred_element_type=F32)
        return [jnp.where(keep, s[:, c * LANE:(c + 1) * LANE], -jnp.inf) for c in range(k2.shape[0] // LANE)]

    def accumulate(blocks, v_list, first):
        m_cur = blocks[0]
        for blk in blocks[1:]:
            m_cur = jnp.maximum(m_cur, blk)
        m_new = jnp.max(m_cur, axis=-1, keepdims=True)
        if not first:
            m_prev = m_ref[...]
            m_new = jnp.maximum(m_prev, m_new)
        pr = [jnp.exp2(blk - m_new) for blk in blocks]
        l_add = pr[0]
        for e in pr[1:]:
            l_add = l_add + e
        l_add = jnp.sum(l_add, axis=-1, keepdims=True)
        per_v = len(blocks) // len(v_list)
        pv = None
        for jj, vv in enumerate(v_list):
            part = jnp.dot(jnp.concatenate(pr[jj * per_v:(jj + 1) * per_v], axis=1), vv, preferred_element_type=F32)
            pv = part if pv is None else pv + part
        if first:
            l_ref[...] = l_add
            acc_ref[...] = pv
        else:
            alpha = jnp.exp2(m_prev - m_new)
            l_ref[...] = alpha * l_ref[...] + l_add
            acc_ref[...] = alpha * acc_ref[...] + pv
        m_ref[...] = m_new

    @pl.when(g == 0)
    def _new_tokens():
        for h in range(n_heads):
            q = q_ref[0, :, h * dv:(h + 1) * dv].astype(F32)
            q_up = jnp.where((row_q < t_new) & (lane_q < dh), q, 0.0)
            q_dn = jnp.where((row_q >= t_new) & (lane_q >= dh), pltpu.roll(q, t_new, 0), 0.0)
            qall_ref[h * rows:(h + 1) * rows, :] = q_up + q_dn
        pad = jnp.zeros((LANE - rows * n_heads, dv), F32)
        kn = jnp.concatenate([kn_ref[0].reshape(rows * n_heads, dv), pad], axis=0)
        vn = jnp.concatenate([vn_ref[0].reshape(rows * n_heads, dv), pad], axis=0)
        r = lax.broadcasted_iota(I32, (n_heads * rows, LANE), 0) & (rows - 1)
        tok_q = jnp.where(r >= t_new, r - t_new, r)
        tok_k = lax.broadcasted_iota(I32, (n_heads * rows, LANE), 1) >> (n_heads.bit_length() - 1)
        accumulate(scores(kn, own_head & (tok_k <= tok_q)), [vn], True)

    blocks = []
    for kr in k_refs:
        blocks += scores(kr[...].reshape(kv_rows, dv), own_head)
    accumulate(blocks, [vr[...].reshape(kv_rows, dv) for vr in v_refs], False)

    @pl.when(g == pl.num_programs(1) - 1)
    def _finish():
        lam = _lambda(lq1_ref, lk1_ref, lq2_ref, lk2_ref, lam_init)
        a_all = acc_ref[...] / l_ref[...]
        for h in range(n_heads):
            a = a_all[h * rows:(h + 1) * rows]
            o = a - lam * pltpu.roll(a, t_new, 0)
            o = _rms(o, gsub_ref[...]) * (1.0 - lam_init)
            o_ref[0, :, h * dv:(h + 1) * dv] = jnp.where(row_q < t_new, o, 0.0)


def _diff_attention_decode(q, k_new, v_new, cache_k, cache_v, layer, page_table, lams, g_sub, lam_init, t_new):
    batch, rows, width = q.shape
    assert rows == 2 * t_new == SUBLANE, "decode kernel stacks the two softmax maps of the new tokens in one sublane tile"
    dv = width // N_DIFF_HEADS
    dh = dv // 2
    n_pages = page_table.shape[1]
    page = cache_k.shape[2]
    n_pg = 8 if n_pages % 8 == 0 else 1
    steps = n_pages // n_pg
    seq_blk = lambda b, g, pt: (b, 0, 0)
    seq_blk4 = lambda b, g, pt: (b, 0, 0, 0)
    fixed = lambda b, g, pt: (0, 0)

    def page_spec(jj):
        return pl.BlockSpec((None, None, page, N_DIFF_HEADS, dv),
                            lambda b, g, pt: (layer, pt[b * n_pages + g * n_pg + jj], 0, 0, 0))

    grid_spec = pltpu.PrefetchScalarGridSpec(
        num_scalar_prefetch=1,
        grid=(batch, steps),
        in_specs=[pl.BlockSpec((1, rows, width), seq_blk)]
                 + [pl.BlockSpec((1, rows, N_DIFF_HEADS, dv), seq_blk4)] * 2
                 + [pl.BlockSpec((1, dh), fixed)] * 4 + [pl.BlockSpec((1, dv), fixed)]
                 + [page_spec(jj) for jj in range(n_pg)] * 2,
        out_specs=pl.BlockSpec((1, rows, width), seq_blk),
        scratch_shapes=[pltpu.VMEM((N_DIFF_HEADS * rows, 1), F32), pltpu.VMEM((N_DIFF_HEADS * rows, 1), F32),
                        pltpu.VMEM((N_DIFF_HEADS * rows, dv), F32), pltpu.VMEM((N_DIFF_HEADS * rows, dv), F32)],
    )
    return pl.pallas_call(
        functools.partial(_diff_decode_kernel, n_pg=n_pg, t_new=t_new, dh=dh, lam_init=lam_init),
        grid_spec=grid_spec,
        out_shape=jax.ShapeDtypeStruct((batch, rows, width), F32),
        compiler_params=_cparams(("parallel", "arbitrary"), 40),
        name="diff_attention_decode",
    )(page_table.reshape(-1), q, k_new, v_new, *lams, g_sub, *([cache_k] * n_pg), *([cache_v] * n_pg))


def _gla_chunk(q, k, v, g, s_t, chunk):
    nb = chunk // GLA_SUB
    row1 = lax.broadcasted_iota(I32, (chunk, 1), 0)
    b = g
    step = 1
    while step < chunk:
        b = b + jnp.where(row1 >= step, pltpu.roll(b, step, 0), 0.0)
        step *= 2
    b_last = b[chunk - 1:chunk, :]

    qe = (q * jnp.exp(b)).astype(BF16)
    o = lax.dot_general(qe, s_t.astype(BF16), _NT, preferred_element_type=F32)
    kd = (k * jnp.exp(b_last - b)).astype(BF16)
    s_new = s_t * jnp.exp(b_last) + lax.dot_general(v.astype(BF16), kd, _TN, preferred_element_type=F32)

    row = lax.broadcasted_iota(I32, (chunk, chunk), 0)
    col = lax.broadcasted_iota(I32, (chunk, chunk), 1)
    row_in_blk = row1 & (GLA_SUB - 1)

    att = jnp.where(col == row, jnp.sum(q * k, axis=-1, keepdims=True), 0.0)
    for d in range(1, min(GLA_SUB, chunk)):
        k_sh = pltpu.roll(k, d, 0)
        b_sh = pltpu.roll(b, d, 0)
        decay = jnp.exp(jnp.where(row_in_blk >= d, b - b_sh, -jnp.inf))
        w = jnp.sum(q * k_sh * decay, axis=-1, keepdims=True)
        att = att + jnp.where(col == row - d, w, 0.0)

    dk = q.shape[1]
    for j in range(nb - 1):
        r0 = (j + 1) * GLA_SUB
        b_end = b[r0 - 1:r0, :]
        kj = k[r0 - GLA_SUB:r0] * jnp.exp(b_end - b[r0 - GLA_SUB:r0])
        above = [jnp.zeros((r0 - GLA_SUB, dk), F32)] if j else []
        kj = jnp.concatenate(above + [kj, jnp.zeros((chunk - r0, dk), F32)], axis=0)
        qj = (q[r0:] * jnp.exp(b[r0:] - b_end)).astype(BF16)
        a = lax.dot_general(qj, kj.astype(BF16), _NT, preferred_element_type=F32)
        att = att + jnp.concatenate([jnp.zeros((r0, chunk), F32), a], axis=0)

    o = o + jnp.dot(att.astype(BF16), v.astype(BF16), preferred_element_type=F32)
    return o, s_new


def _gla_kernel(gqk_ref, gv_ref, gr_ref, glog_ref, s0_ref, ggla_ref, o_ref, s_ref, st_ref,
                *, chunk, t_valid, dk, dv, scale):
    tb = pl.program_id(1)
    t_blk = gqk_ref.shape[1]
    n_heads = N_GLA_HEADS

    @pl.when(tb == 0)
    def _load_state():
        for h in range(n_heads):
            st_ref[h] = s0_ref[0, h].T

    g_gla = ggla_ref[...]

    def body(c, carry):
        r0 = pl.multiple_of(c * chunk, chunk)
        rows = pl.ds(r0, chunk)
        pos = tb * t_blk + r0 + lax.broadcasted_iota(I32, (chunk, 1), 0)
        live = pos < t_valid
        for h in range(n_heads):
            q = gqk_ref[0, rows, h * dk:(h + 1) * dk] * scale
            k = gqk_ref[0, rows, (n_heads + h) * dk:(n_heads + h + 1) * dk]
            v = gv_ref[0, rows, h * dv:(h + 1) * dv]
            g = jnp.where(live, glog_ref[0, rows, h * dk:(h + 1) * dk], 0.0)
            k = jnp.where(live, k, 0.0)
            o, s_new = _gla_chunk(q, k, v, g, st_ref[h], chunk)
            st_ref[h] = s_new
            gr = gr_ref[0, rows, h * dv:(h + 1) * dv]
            o_ref[0, rows, h * dv:(h + 1) * dv] = (_rms(o, g_gla) * (gr * _sigmoid(gr))).astype(o_ref.dtype)
        return carry

    lax.fori_loop(0, t_blk // chunk, body, 0)

    @pl.when(tb == pl.num_programs(1) - 1)
    def _store_state():
        for h in range(n_heads):
            s_ref[0, h] = st_ref[h].T


def _gla(gqk, gv, gr, glog, s0, g_gla, t_valid, out_dtype):
    batch, seq, _ = gqk.shape
    dk, dv = s0.shape[2], s0.shape[3]
    chunk = next(c for c in (64, 32, 16, 8) if seq % c == 0)
    t_blk = _tile(seq, 256)
    tok = lambda b, t: (b, t, 0)
    return pl.pallas_call(
        functools.partial(_gla_kernel, chunk=chunk, t_valid=t_valid, dk=dk, dv=dv, scale=dk ** -0.5),
        grid=(batch, seq // t_blk),
        in_specs=[
            pl.BlockSpec((1, t_blk, gqk.shape[2]), tok),
            pl.BlockSpec((1, t_blk, gv.shape[2]), tok),
            pl.BlockSpec((1, t_blk, gr.shape[2]), tok),
            pl.BlockSpec((1, t_blk, glog.shape[2]), tok),
            pl.BlockSpec((1, N_GLA_HEADS, dk, dv), lambda b, t: (b, 0, 0, 0)),
            pl.BlockSpec((1, dv), lambda b, t: (0, 0)),
        ],
        out_specs=[
            pl.BlockSpec((1, t_blk, gv.shape[2]), tok),
            pl.BlockSpec((1, N_GLA_HEADS, dk, dv), lambda b, t: (b, 0, 0, 0)),
        ],
        out_shape=[jax.ShapeDtypeStruct(gv.shape, out_dtype), jax.ShapeDtypeStruct(s0.shape, F32)],
        scratch_shapes=[pltpu.VMEM((N_GLA_HEADS, dv, dk), F32)],
        compiler_params=_cparams(("parallel", "arbitrary"), 32),
        name="gla",
    )(gqk, gv, gr, glog, s0, g_gla)


def _cross_kernel(q_ref, mk_ref, mv_ref, o_ref, *, dh):
    for h in range(N_MEM_HEADS):
        sl = slice(h * dh, (h + 1) * dh)
        s = lax.dot_general(q_ref[0, :, sl].astype(BF16), mk_ref[0, :, h, :].astype(BF16), _NT,
                            preferred_element_type=F32)
        pr = jnp.exp(s - jnp.max(s, axis=-1, keepdims=True))
        pr = pr / jnp.sum(pr, axis=-1, keepdims=True)
        o_ref[0, :, sl] = jnp.dot(pr.astype(BF16), mv_ref[0, :, h, :].astype(BF16),
                                  preferred_element_type=F32).astype(o_ref.dtype)


def _cross_attention(q, mk, mv, mem_index):
    batch, seq, d = q.shape
    mem, n_heads, dh = mk.shape[-3:]
    tq = _tile(seq, 512)
    lead = mk.ndim - 4
    mem_spec = pl.BlockSpec((None,) * lead + (1, mem, n_heads, dh), lambda b, t: mem_index(b) + (0, 0, 0))
    return pl.pallas_call(
        functools.partial(_cross_kernel, dh=dh),
        grid=(batch, seq // tq),
        in_specs=[pl.BlockSpec((1, tq, d), lambda b, t: (b, t, 0)), mem_spec, mem_spec],
        out_specs=pl.BlockSpec((1, tq, d), lambda b, t: (b, t, 0)),
        out_shape=jax.ShapeDtypeStruct(q.shape, BF16 if tq % 16 == 0 else F32),
        compiler_params=_cparams(("parallel", "arbitrary"), 40),
        name="cross_attention",
    )(q, mk, mv)


def _route_kernel(x_ref, g_ref, wh_ref, wl_ref, xn_ref, idx_ref, wgt_ref):
    xn = _rms(x_ref[...], g_ref[...])
    xn_ref[...] = xn
    xh = xn.astype(BF16)
    xl = (xn - xh.astype(F32)).astype(BF16)
    logits = (jnp.dot(xh, wh_ref[...], preferred_element_type=F32)
              + jnp.dot(xl, wh_ref[...], preferred_element_type=F32)
              + jnp.dot(xh, wl_ref[...], preferred_element_type=F32))
    lane = lax.broadcasted_iota(I32, logits.shape, 1)
    lane_f = lane.astype(F32)
    neg = -jnp.inf

    def first_max(vals):
        top = jnp.max(vals, axis=-1, keepdims=True)
        first = jnp.min(jnp.where(vals == top, lane_f, float(LANE)), axis=-1, keepdims=True)
        return top, first.astype(I32)

    is_grp = lane < N_GROUPS
    g_top, g_idx = first_max(jnp.where(is_grp, logits, neg))
    g_w = 1.0 / jnp.sum(jnp.where(is_grp, jnp.exp(logits - g_top), 0.0), axis=-1, keepdims=True)
    lo = N_GROUPS + g_idx * EXPERTS_PER_GROUP
    le = jnp.where((lane >= lo) & (lane < lo + EXPERTS_PER_GROUP), logits, neg)
    v1, i1 = first_max(le)
    v2, i2 = first_max(jnp.where(lane == i1, neg, le))
    e2 = jnp.exp(v2 - v1)
    w1 = g_w / (1.0 + e2)
    w2 = g_w * e2 / (1.0 + e2)
    idx_ref[...] = jnp.where(lane == 0, i1 - N_GROUPS, jnp.where(lane == 1, i2 - N_GROUPS, 0))
    wgt_ref[...] = jnp.where(lane == 0, w1, jnp.where(lane == 1, w2, 0.0))


def _route(x, g_ffn, w_hi, w_lo):
    n, d = x.shape
    tm = _tile(n, 512)
    row = lambda i: (i, 0)
    fixed = lambda i: (0, 0)
    return pl.pallas_call(
        _route_kernel,
        grid=(n // tm,),
        in_specs=[pl.BlockSpec((tm, d), row), pl.BlockSpec((1, d), fixed),
                  pl.BlockSpec((d, LANE), fixed), pl.BlockSpec((d, LANE), fixed)],
        out_specs=[pl.BlockSpec((tm, d), row), pl.BlockSpec((tm, LANE), row), pl.BlockSpec((tm, LANE), row)],
        out_shape=[jax.ShapeDtypeStruct((n, d), F32), jax.ShapeDtypeStruct((n, LANE), I32),
                   jax.ShapeDtypeStruct((n, LANE), F32)],
        compiler_params=_cparams(("parallel",), 40),
        name="moe_route",
    )(x, g_ffn, w_hi, w_lo)


def _row_copies(n_rows, make_copy):
    def start(it, c):
        for u in range(DMA_UNROLL):
            for kk in range(2):
                make_copy(it * DMA_UNROLL + u, kk).start(priority=kk)
        return c

    def wait(it, c):
        for u in range(DMA_UNROLL):
            for kk in range(2):
                make_copy(it * DMA_UNROLL + u, kk).wait()
        return c

    assert n_rows % DMA_UNROLL == 0
    lax.fori_loop(0, n_rows // DMA_UNROLL, start, 0)
    lax.fori_loop(0, n_rows // DMA_UNROLL, wait, 0)


def _dispatch_kernel(pos_ref, pend_ref, x_ref, xs_ref, zero_ref, sem, zsem, *, tile_e):
    i = pl.program_id(0)
    tm = x_ref.shape[0]

    @pl.when(i == 0)
    def _zero_padding():
        zero_ref[...] = jnp.zeros(zero_ref.shape, F32)

        def fill(e):
            start = pl.multiple_of(jnp.maximum(pend_ref[e] - tile_e, 0), tile_e)
            return pltpu.make_async_copy(zero_ref, xs_ref.at[pl.ds(start, tile_e), :], zsem)

        for e in range(N_EXPERTS):
            fill(e).start()
        for e in range(N_EXPERTS):
            fill(e).wait()

        def tail(t):
            return pltpu.make_async_copy(zero_ref, xs_ref.at[pl.ds(pl.multiple_of(t * tile_e, tile_e), tile_e), :], zsem)

        first_unused = pend_ref[N_EXPERTS - 1] // tile_e
        n_tiles = xs_ref.shape[0] // tile_e
        lax.fori_loop(first_unused, n_tiles, lambda t, c: (tail(t).start(), c)[1], 0)
        lax.fori_loop(first_unused, n_tiles, lambda t, c: (tail(t).wait(), c)[1], 0)

    def row_copy(r, kk):
        dst = pos_ref[(i * tm + r) * 2 + kk]
        return pltpu.make_async_copy(x_ref.at[pl.ds(r, 1), :], xs_ref.at[pl.ds(dst, 1), :], sem.at[kk])

    _row_copies(tm, row_copy)


def _dispatch(xn, pos, p_end, n_slots, tile_e):
    n, d = xn.shape
    tm = _tile(n, 256)
    grid_spec = pltpu.PrefetchScalarGridSpec(
        num_scalar_prefetch=2,
        grid=(n // tm,),
        in_specs=[pl.BlockSpec((tm, d), lambda i, pos, pend: (i, 0))],
        out_specs=pl.BlockSpec(memory_space=pl.ANY),
        scratch_shapes=[pltpu.VMEM((tile_e, d), F32), pltpu.SemaphoreType.DMA((2,)), pltpu.SemaphoreType.DMA(())],
    )
    return pl.pallas_call(
        functools.partial(_dispatch_kernel, tile_e=tile_e),
        grid_spec=grid_spec,
        out_shape=jax.ShapeDtypeStruct((n_slots, d), F32),
        compiler_params=_cparams(("arbitrary",), 32),
        name="moe_dispatch",
    )(pos, p_end, xn)


def _experts_kernel(te_ref, nu_ref, x_ref, wg_ref, wu_ref, wd_ref, y_ref, wgb_ref, wub_ref, wdb_ref):
    t = pl.program_id(0)
    prev = te_ref[jnp.maximum(t - 1, 0)]

    @pl.when((t == 0) | (te_ref[t] != prev))
    def _cast_weights():
        wgb_ref[...] = wg_ref[...].astype(BF16)
        wub_ref[...] = wu_ref[...].astype(BF16)
        wdb_ref[...] = wd_ref[...].astype(BF16)

    @pl.when(t < nu_ref[0])
    def _mlp():
        x = x_ref[...].astype(BF16)
        hg = jnp.dot(x, wgb_ref[...], preferred_element_type=F32)
        hu = jnp.dot(x, wub_ref[...], preferred_element_type=F32)
        h = (hg * _sigmoid(hg)) * hu
        y_ref[...] = jnp.dot(h.astype(BF16), wdb_ref[...], preferred_element_type=F32)

    @pl.when(t >= nu_ref[0])
    def _unused():
        y_ref[...] = jnp.zeros(y_ref.shape, F32)


def _experts(xs, tile_expert, n_used, w_gate, w_up, w_down, tm):
    n_slots, d = xs.shape
    de = w_gate.shape[-1]
    lead = w_gate.ndim - 3
    layer_idx = w_gate.shape[:lead]
    assert all(s == 1 for s in layer_idx)
    wspec = lambda r, c: pl.BlockSpec((None,) * (lead + 1) + (r, c), lambda t, te, nu: (0,) * lead + (te[t], 0, 0))
    grid_spec = pltpu.PrefetchScalarGridSpec(
        num_scalar_prefetch=2,
        grid=(n_slots // tm,),
        in_specs=[
            pl.BlockSpec((tm, d), lambda t, te, nu: (jnp.minimum(t, nu[0] - 1), 0)),
            wspec(d, de), wspec(d, de), wspec(de, d),
        ],
        out_specs=pl.BlockSpec((tm, d), lambda t, te, nu: (t, 0)),
        scratch_shapes=[pltpu.VMEM((d, de), BF16), pltpu.VMEM((d, de), BF16), pltpu.VMEM((de, d), BF16)],
    )
    return pl.pallas_call(
        _experts_kernel,
        grid_spec=grid_spec,
        out_shape=jax.ShapeDtypeStruct((n_slots, d), F32),
        compiler_params=_cparams(("arbitrary",), 52),
        name="moe_experts",
    )(tile_expert, n_used, xs, w_gate, w_up, w_down)


def _combine_kernel(pos_ref, x_ref, wgt_ref, g_ref, ys_ref, o_ref, buf_ref, sem):
    i = pl.program_id(0)
    tm = x_ref.shape[0]

    def row_copy(r, kk):
        src = pos_ref[(i * tm + r) * 2 + kk]
        return pltpu.make_async_copy(ys_ref.at[pl.ds(src, 1), :], buf_ref.at[kk, pl.ds(r, 1), :], sem.at[kk])

    _row_copies(tm, row_copy)
    wgt = wgt_ref[...]
    y = x_ref[...] + (wgt[:, 0:1] * buf_ref[0] + wgt[:, 1:2] * buf_ref[1])
    o_ref[...] = _rms(y, g_ref[...])


def _combine(x, wgt, g_final, ys, pos):
    n, d = x.shape
    tm = _tile(n, 256)
    grid_spec = pltpu.PrefetchScalarGridSpec(
        num_scalar_prefetch=1,
        grid=(n // tm,),
        in_specs=[pl.BlockSpec((tm, d), lambda i, pos: (i, 0)), pl.BlockSpec((tm, LANE), lambda i, pos: (i, 0)),
                  pl.BlockSpec((1, d), lambda i, pos: (0, 0)), pl.BlockSpec(memory_space=pl.ANY)],
        out_specs=pl.BlockSpec((tm, d), lambda i, pos: (i, 0)),
        scratch_shapes=[pltpu.VMEM((2, tm, d), F32), pltpu.SemaphoreType.DMA((2,))],
    )
    return pl.pallas_call(
        _combine_kernel,
        grid_spec=grid_spec,
        out_shape=jax.ShapeDtypeStruct((n, d), F32),
        compiler_params=_cparams(("arbitrary",), 32),
        name="moe_combine",
    )(pos, x, wgt, g_final, ys)


def _moe_and_final_norm(x, g_ffn, w_route_hi, w_route_lo, w_gate, w_up, w_down, g_final):
    n, d = x.shape
    xn, ridx, rw = _route(x, g_ffn, w_route_hi, w_route_lo)
    tm = 256 if n >= 2048 else 128
    e = ridx[:, :2].reshape(-1)
    onehot = (e[:, None] == jnp.arange(N_EXPERTS, dtype=I32)[None, :]).astype(I32)
    csum = jnp.cumsum(onehot, axis=0)
    counts = csum[-1]
    padded = (counts + tm - 1) // tm * tm
    p_end = jnp.cumsum(padded).astype(I32)
    pos = jnp.sum(onehot * ((p_end - padded)[None, :] + csum - 1), axis=1).astype(I32)
    n_tiles = (2 * n + N_EXPERTS * (tm - 1)) // tm + 1
    tile_start = jnp.arange(n_tiles, dtype=I32) * tm
    tile_expert = jnp.minimum(jnp.sum((tile_start[:, None] >= p_end[None, :]).astype(I32), axis=1), N_EXPERTS - 1)
    n_used = (p_end[-1:] // tm).astype(I32)
    xs = _dispatch(xn, pos, p_end, n_tiles * tm, tm)
    ys = _experts(xs, tile_expert.astype(I32), n_used, w_gate, w_up, w_down, tm)
    return _combine(x, rw, g_final, ys, pos)


def _layer(x3, s0, t_valid, mk, mv, mem_index, diff_fn, lw):
    batch, seq, d = x3.shape
    n = batch * seq
    x = x3.reshape(n, d)
    qs, k4, v4, kb, vb, gqk, gv, gr, glog = _in_projection(x, lw["g_mix"], lw["w_in_main"], lw["w_in_ga"],
                                                            lw["w_a2p"], lw["b_a"])
    d_out = diff_fn(qs, k4, v4, kb, vb)
    r3 = lambda a: a.reshape(batch, seq, a.shape[-1])
    g_out, s_new = _gla(r3(gqk), r3(gv), r3(gr), r3(glog), s0, lw["g_gla"], t_valid,
                        BF16 if seq % 16 == 0 else F32)
    x1 = _matmul_residual([d_out.reshape(n, -1), g_out.reshape(n, -1)], lw["w_o"], x, "mixer_out")
    dh_mem = d // N_MEM_HEADS
    xq = _norm_matmul(x1, lw["g_cross"], lw["w_q_mem"], BF16 if seq % 16 == 0 else F32, "cross_q",
                      out_scale=dh_mem ** -0.5)
    ca = _cross_attention(xq.reshape(batch, seq, d), mk, mv, mem_index)
    x2 = _matmul_residual([ca.reshape(n, d)], lw["w_o_mem"], x1, "cross_out")
    return x2, k4, v4, s_new


def kernel(x_prompt, x_sample, mem_prompt, cache_k, cache_v, state_gla, cache_mem_k, cache_mem_v, page_table, g_mix, w_in, w_a2, b_a, lambda_q1, lambda_k1, lambda_q2, lambda_k2, g_subln, g_gla, w_o, g_mem, g_cross, w_q_mem, w_k_mem, w_v_mem, w_o_mem, g_ffn, w_group, w_router, w_gate, w_up, w_down, g_final):
    depth = w_in.shape[0]
    batch, seq, d = x_prompt.shape
    dec_batch, dec_seq, _ = x_sample.shape
    mem_len = mem_prompt.shape[1]
    width = d // 2
    main_cols = w_in.shape[2] - GATE_RANK
    assert main_cols == 6 * width and w_a2.shape[1:] == (GATE_RANK, width // 2)
    dk_gla = width // N_GLA_HEADS // 2
    dv_gla = width // N_GLA_HEADS
    dv_diff = width // N_DIFF_HEADS
    dec_pad = -(-dec_seq // SUBLANE) * SUBLANE

    xp = x_prompt
    xs = jnp.pad(x_sample, ((0, 0), (0, dec_pad - dec_seq), (0, 0)))
    s0_prompt = jnp.zeros((batch, N_GLA_HEADS, dk_gla, dv_gla), F32)
    row2 = lambda a: a.reshape(1, -1)
    outs = {name: [] for name in ("kp", "vp", "sp", "mkp", "mvp", "ks", "vs", "ss")}
    for l in range(depth):
        lam_init = 0.8 - 0.6 * math.exp(-0.3 * l)
        lw = dict(
            g_mix=row2(g_mix[l]), g_gla=row2(g_gla[l]), g_cross=row2(g_cross[l]), b_a=row2(b_a[l]),
            w_in_main=w_in[l][:, :main_cols].astype(BF16),
            w_in_ga=jnp.pad(w_in[l][:, main_cols:], ((0, 0), (0, LANE - GATE_RANK))).astype(BF16),
            w_a2p=jnp.pad(w_a2[l], ((0, LANE - GATE_RANK), (0, 0))).astype(BF16),
            w_o=w_o[l].astype(BF16), w_q_mem=w_q_mem[l].astype(BF16), w_o_mem=w_o_mem[l].astype(BF16),
        )
        lams = (row2(lambda_q1[l]), row2(lambda_k1[l]), row2(lambda_q2[l]), row2(lambda_k2[l]))
        g_sub = row2(g_subln[l])
        w_route = jnp.pad(jnp.concatenate([w_group[l], w_router[l]], axis=1),
                          ((0, 0), (0, LANE - N_GROUPS - N_EXPERTS)))
        w_route_hi = w_route.astype(BF16)
        w_route_lo = (w_route - w_route_hi.astype(F32)).astype(BF16)

        mem2 = mem_prompt.reshape(batch * mem_len, d)
        mkp = _norm_matmul(mem2, row2(g_mem[l]), w_k_mem[l].astype(BF16), F32, "mem_k", n_heads=N_MEM_HEADS)
        mvp = _norm_matmul(mem2, row2(g_mem[l]), w_v_mem[l].astype(BF16), F32, "mem_v", n_heads=N_MEM_HEADS)
        mkp = mkp.reshape(batch, mem_len, N_MEM_HEADS, -1)
        mvp = mvp.reshape(batch, mem_len, N_MEM_HEADS, -1)

        prompt_attn = lambda qs, k4, v4, kb, vb: _diff_attention_prompt(qs, kb, vb, lams, g_sub, lam_init, batch, seq,
                                                                        BF16)
        xp2, kp, vp, sp = _layer(xp, s0_prompt, seq, mkp, mvp, lambda b: (b,), prompt_attn, lw)

        sample_attn = lambda qs, k4, v4, kb, vb, l=l: _diff_attention_decode(
            qs.reshape(dec_batch, dec_pad, width), k4.reshape(dec_batch, dec_pad, N_DIFF_HEADS, dv_diff),
            v4.reshape(dec_batch, dec_pad, N_DIFF_HEADS, dv_diff), cache_k, cache_v, l, page_table, lams, g_sub,
            lam_init, dec_seq)
        xs2, ks, vs, ss = _layer(xs, state_gla[l], dec_seq, cache_mem_k, cache_mem_v, lambda b, l=l: (l, b),
                                 sample_attn, lw)

        assert l == depth - 1, "the MoE combine kernel fuses the final norm; deeper stacks need an un-normed variant"
        moe = functools.partial(_moe_and_final_norm, g_ffn=row2(g_ffn[l]), w_route_hi=w_route_hi,
                                w_route_lo=w_route_lo, w_gate=w_gate[l:l + 1], w_up=w_up[l:l + 1],
                                w_down=w_down[l:l + 1], g_final=row2(g_final))
        xp = moe(xp2).reshape(batch, seq, d)
        xs = moe(xs2).reshape(dec_batch, dec_pad, d)

        outs["kp"].append(kp.reshape(batch, seq, N_DIFF_HEADS, dv_diff))
        outs["vp"].append(vp.reshape(batch, seq, N_DIFF_HEADS, dv_diff))
        outs["sp"].append(sp)
        outs["mkp"].append(mkp)
        outs["mvp"].append(mvp)
        outs["ks"].append(ks.reshape(dec_batch, dec_pad, N_DIFF_HEADS, dv_diff)[:, :dec_seq])
        outs["vs"].append(vs.reshape(dec_batch, dec_pad, N_DIFF_HEADS, dv_diff)[:, :dec_seq])
        outs["ss"].append(ss)

    st = lambda name: jnp.stack(outs[name])
    return (xp, xs[:, :dec_seq], st("kp"), st("vp"), st("sp"), st("mkp"), st("mvp"), st("ks"), st("vs"), st("ss"))
```

```python
import functools
import math

import jax
import jax.numpy as jnp
from jax import lax
from jax.experimental import pallas as pl
from jax.experimental.pallas import tpu as pltpu

F32 = jnp.float32
BF16 = jnp.bfloat16
I32 = jnp.int32

N_DIFF_HEADS = 4
N_GLA_HEADS = 4
GATE_RANK = 16
GATE_TAU = 16.0
N_MEM_HEADS = 4
N_GROUPS = 4
EXPERTS_PER_GROUP = 8
N_EXPERTS = N_GROUPS * EXPERTS_PER_GROUP
RMS_EPS = 1e-6
GLA_SUB = 8
LOG2E = math.log2(math.e)

LANE = 128
SUBLANE = 8
MIB = 1024 * 1024
DMA_UNROLL = 8

_NT = (((1,), (1,)), ((), ()))
_TN = (((0,), (0,)), ((), ()))


def _cparams(semantics, vmem_mib):
    return pltpu.CompilerParams(dimension_semantics=semantics, vmem_limit_bytes=int(vmem_mib * MIB))


def _tile(n, target):
    if n <= target:
        return n
    t = target - target % SUBLANE
    while t >= SUBLANE:
        if n % t == 0:
            return t
        t -= SUBLANE
    raise ValueError(f"no sublane-aligned tile of {n} below {target}")


def _rms(x, g):
    ms = jnp.mean(x * x, axis=-1, keepdims=True)
    return x * lax.rsqrt(ms + RMS_EPS) * g


def _sigmoid(x):
    return 1.0 / (1.0 + jnp.exp(-x))


def _resident(a):
    return pl.BlockSpec(a.shape, lambda *_: (0,) * a.ndim, pipeline_mode=pl.Buffered(1))


def _store_heads(o_ref, val, n_heads):
    w = val.shape[1] // n_heads
    for h in range(n_heads):
        o_ref[:, h, :] = val[:, h * w:(h + 1) * w].astype(o_ref.dtype)


def _inproj_kernel(x_ref, g_ref, w_ref, wga_ref, wa2_ref, ba_ref,
                   qs_ref, k_ref, v_ref, kb_ref, vb_ref, gqk_ref, gv_ref, gr_ref, glog_ref, *, q_scale):
    xn = _rms(x_ref[...], g_ref[...]).astype(BF16)
    wblk = qs_ref.shape[1]

    def proj(group):
        return jnp.dot(xn, w_ref[:, group * wblk:(group + 1) * wblk], preferred_element_type=F32)

    qs_ref[...] = (proj(0) * q_scale).astype(BF16)
    for group, (o_ref, ob_ref) in ((1, (k_ref, kb_ref)), (2, (v_ref, vb_ref))):
        r = proj(group)
        _store_heads(o_ref, r, N_DIFF_HEADS)
        ob_ref[...] = r.astype(BF16)
    for group, o_ref in ((3, gqk_ref), (4, gv_ref), (5, gr_ref)):
        o_ref[...] = proj(group)
    ga = jnp.dot(xn, wga_ref[...], preferred_element_type=F32)
    z = jnp.dot(ga.astype(BF16), wa2_ref[...], preferred_element_type=F32) + ba_ref[...]
    glog_ref[...] = (jnp.minimum(z, 0.0) - jnp.log1p(jnp.exp(-jnp.abs(z)))) * (1.0 / GATE_TAU)


def _in_projection(x, g_mix, w_main, w_ga, w_a2p, b_a):
    n, d = x.shape
    wblk = d // 2
    assert w_main.shape[1] == 6 * wblk
    dv = wblk // N_DIFF_HEADS
    gk = w_a2p.shape[1]
    tm = _tile(n, 256)
    row = lambda i: (i, 0)
    row3 = lambda i: (i, 0, 0)
    flat = lambda dt: (jax.ShapeDtypeStruct((n, wblk), dt), pl.BlockSpec((tm, wblk), row))
    heads = (jax.ShapeDtypeStruct((n, N_DIFF_HEADS, dv), F32), pl.BlockSpec((tm, N_DIFF_HEADS, dv), row3))
    outs = [flat(BF16), heads, heads, flat(BF16), flat(BF16), flat(F32), flat(F32), flat(F32),
            (jax.ShapeDtypeStruct((n, gk), F32), pl.BlockSpec((tm, gk), row))]
    return pl.pallas_call(
        functools.partial(_inproj_kernel, q_scale=(dv // 2) ** -0.5 * LOG2E),
        grid=(n // tm,),
        in_specs=[pl.BlockSpec((tm, d), row), _resident(g_mix), _resident(w_main), _resident(w_ga),
                  _resident(w_a2p), _resident(b_a)],
        out_specs=[o[1] for o in outs],
        out_shape=[o[0] for o in outs],
        compiler_params=_cparams(("parallel",), 56),
        name="in_projection",
    )(x, g_mix, w_main, w_ga, w_a2p, b_a)


def _norm_mm_kernel(x_ref, g_ref, w_ref, o_ref, *, out_scale, n_heads):
    xn = _rms(x_ref[...], g_ref[...]).astype(BF16)
    r = jnp.dot(xn, w_ref[...], preferred_element_type=F32)
    if out_scale != 1.0:
        r = r * out_scale
    if n_heads:
        _store_heads(o_ref, r, n_heads)
    else:
        o_ref[...] = r.astype(o_ref.dtype)


def _norm_matmul(x, g, w, out_dtype, name, out_scale=1.0, n_heads=0):
    n, d = x.shape
    dout = w.shape[1]
    tm = _tile(n, 512)
    if n_heads:
        out_spec = pl.BlockSpec((tm, n_heads, dout // n_heads), lambda i: (i, 0, 0))
        out_shape = jax.ShapeDtypeStruct((n, n_heads, dout // n_heads), out_dtype)
    else:
        out_spec = pl.BlockSpec((tm, dout), lambda i: (i, 0))
        out_shape = jax.ShapeDtypeStruct((n, dout), out_dtype)
    return pl.pallas_call(
        functools.partial(_norm_mm_kernel, out_scale=out_scale, n_heads=n_heads),
        grid=(n // tm,),
        in_specs=[pl.BlockSpec((tm, d), lambda i: (i, 0)), _resident(g), _resident(w)],
        out_specs=out_spec,
        out_shape=out_shape,
        compiler_params=_cparams(("parallel",), 48),
        name=name,
    )(x, g, w)


def _mm_res_kernel(*refs, widths):
    a_refs = refs[:len(widths)]
    w_ref, res_ref, o_ref = refs[len(widths):]
    acc = res_ref[...]
    k0 = 0
    for a_ref, kw in zip(a_refs, widths):
        acc = acc + jnp.dot(a_ref[...].astype(BF16), w_ref[k0:k0 + kw, :], preferred_element_type=F32)
        k0 += kw
    o_ref[...] = acc


def _matmul_residual(pieces, w, res, name):
    n, dout = res.shape
    widths = tuple(p.shape[1] for p in pieces)
    assert sum(widths) == w.shape[0]
    tm = _tile(n, 512)
    in_specs = [pl.BlockSpec((tm, kw), lambda i: (i, 0)) for kw in widths]
    in_specs += [_resident(w), pl.BlockSpec((tm, dout), lambda i: (i, 0))]
    return pl.pallas_call(
        functools.partial(_mm_res_kernel, widths=widths),
        grid=(n // tm,),
        in_specs=in_specs,
        out_specs=pl.BlockSpec((tm, dout), lambda i: (i, 0)),
        out_shape=jax.ShapeDtypeStruct((n, dout), F32),
        compiler_params=_cparams(("parallel",), 48),
        name=name,
    )(*pieces, w, res)


def _lambda(lq1_ref, lk1_ref, lq2_ref, lk2_ref, lam_init):
    d1 = jnp.sum(lq1_ref[...] * lk1_ref[...], axis=-1, keepdims=True)
    d2 = jnp.sum(lq2_ref[...] * lk2_ref[...], axis=-1, keepdims=True)
    return jnp.exp(d1) - jnp.exp(d2) + lam_init


def _diff_prompt_kernel(qi_ref, ki_ref, q_ref, k_ref, v_ref, lq1_ref, lk1_ref, lq2_ref, lk2_ref, gsub_ref,
                        o_ref, m_ref, l_ref, acc_ref, *, dh, sub, lam_init):
    p = pl.program_id(2)
    qi = qi_ref[p]
    ki = ki_ref[p]
    tq, tk = q_ref.shape[0], k_ref.shape[0]
    dv = 2 * dh

    @pl.when(ki == 0)
    def _init():
        m_ref[...] = jnp.full(m_ref.shape, -jnp.inf, F32)
        l_ref[...] = jnp.zeros(l_ref.shape, F32)
        acc_ref[...] = jnp.zeros(acc_ref.shape, F32)

    def update(r, diagonal):
        rows = slice(r * sub, (r + 1) * sub)
        ncol = (r + 1) * sub if diagonal else tk
        v = v_ref[0:ncol, :]
        if diagonal:
            keep = lax.broadcasted_iota(I32, (sub, sub), 1) <= lax.broadcasted_iota(I32, (sub, sub), 0)
        for mi in range(2):
            s = lax.dot_general(q_ref[rows, mi * dh:(mi + 1) * dh], k_ref[0:ncol, mi * dh:(mi + 1) * dh], _NT,
                                preferred_element_type=F32)
            blocks = [s[:, c * LANE:(c + 1) * LANE] for c in range(ncol // LANE)]
            if diagonal:
                first = ncol // LANE - sub // LANE
                for c in range(sub // LANE):
                    blocks[first + c] = jnp.where(keep[:, c * LANE:(c + 1) * LANE], blocks[first + c], -jnp.inf)
            m_cur = blocks[0]
            for blk in blocks[1:]:
                m_cur = jnp.maximum(m_cur, blk)
            m_prev = m_ref[mi, rows, :]
            m_new = jnp.maximum(m_prev, jnp.max(m_cur, axis=-1, keepdims=True))
            alpha = jnp.exp2(m_prev - m_new)
            l_add = None
            pr = []
            for blk in blocks:
                e = jnp.exp2(blk - m_new)
                l_add = e if l_add is None else l_add + e
                pr.append(e.astype(BF16))
            pv = jnp.dot(jnp.concatenate(pr, axis=1), v, preferred_element_type=F32)
            l_ref[mi, rows, :] = alpha * l_ref[mi, rows, :] + l_add
            acc_ref[mi, rows, :] = jnp.concatenate([alpha] * (dv // LANE), axis=1) * acc_ref[mi, rows, :] + pv
            m_ref[mi, rows, :] = m_new

    @pl.when(ki < qi)
    def _below_diagonal():
        for r in range(tq // sub):
            update(r, False)

    @pl.when(ki == qi)
    def _diagonal():
        lam = _lambda(lq1_ref, lk1_ref, lq2_ref, lk2_ref, lam_init)
        for r in range(tq // sub):
            update(r, True)
            rows = slice(r * sub, (r + 1) * sub)
            l1 = jnp.sum(l_ref[0, rows, :], axis=-1, keepdims=True)
            l2 = jnp.sum(l_ref[1, rows, :], axis=-1, keepdims=True)
            o = acc_ref[0, rows, :] / l1 - lam * (acc_ref[1, rows, :] / l2)
            o_ref[rows, :] = (_rms(o, gsub_ref[...]) * (1.0 - lam_init)).astype(o_ref.dtype)


def _diff_attention_prompt(q, k, v, lams, g_sub, lam_init, batch, seq, out_dtype):
    n, width = q.shape
    dv = width // N_DIFF_HEADS
    dh = dv // 2
    tq = _tile(seq, 1024)
    sub = _tile(tq, 128)
    assert sub % LANE == 0 or tq == sub
    nq = seq // tq
    pairs = [(a, b) for a in range(nq) for b in range(a + 1)]
    qi = jnp.asarray([a for a, _ in pairs], I32)
    ki = jnp.asarray([b for _, b in pairs], I32)
    grid_spec = pltpu.PrefetchScalarGridSpec(
        num_scalar_prefetch=2,
        grid=(batch, N_DIFF_HEADS, len(pairs)),
        in_specs=[
            pl.BlockSpec((tq, dv), lambda b, h, p, qi, ki: (b * nq + qi[p], h)),
            pl.BlockSpec((tq, dv), lambda b, h, p, qi, ki: (b * nq + ki[p], h)),
            pl.BlockSpec((tq, dv), lambda b, h, p, qi, ki: (b * nq + ki[p], h)),
        ] + [pl.BlockSpec((1, dh), lambda b, h, p, qi, ki: (0, 0))] * 4
          + [pl.BlockSpec((1, dv), lambda b, h, p, qi, ki: (0, 0))],
        out_specs=pl.BlockSpec((tq, dv), lambda b, h, p, qi, ki: (b * nq + qi[p], h)),
        scratch_shapes=[pltpu.VMEM((2, tq, LANE), F32), pltpu.VMEM((2, tq, LANE), F32), pltpu.VMEM((2, tq, dv), F32)],
    )
    return pl.pallas_call(
        functools.partial(_diff_prompt_kernel, dh=dh, sub=sub, lam_init=lam_init),
        grid_spec=grid_spec,
        out_shape=jax.ShapeDtypeStruct((n, width), out_dtype),
        compiler_params=_cparams(("parallel", "parallel", "arbitrary"), 40),
        name="diff_attention_prompt",
    )(qi, ki, q, k, v, *lams, g_sub)


def _diff_decode_kernel(pt_ref, q_ref, kn_ref, vn_ref, lq1_ref, lk1_ref, lq2_ref, lk2_ref, gsub_ref, *rest,
                        n_pg, t_new, dh, lam_init):
    k_refs = rest[:n_pg]
    v_refs = rest[n_pg:2 * n_pg]
    o_ref, m_ref, l_ref, acc_ref, qall_ref = rest[2 * n_pg:]
    g = pl.program_id(1)
    n_heads = N_DIFF_HEADS
    dv = 2 * dh
    rows = 2 * t_new
    page = k_refs[0].shape[0]
    kv_rows = page * n_heads
    row_q = lax.broadcasted_iota(I32, (rows, dv), 0)
    lane_q = lax.broadcasted_iota(I32, (rows, dv), 1)

    own_head = ((lax.broadcasted_iota(I32, (n_heads * rows, LANE), 1) & (n_heads - 1))
                == lax.broadcasted_iota(I32, (n_heads * rows, LANE), 0) >> (rows.bit_length() - 1))

    def scores(k2, keep):
        s = lax.dot_general(qall_ref[...], k2, _NT, preferred_element_type=F32)
        return [jnp.where(keep, s[:, c * LANE:(c + 1) * LANE], -jnp.inf) for c in range(k2.shape[0] // LANE)]

    def accumulate(blocks, v_list, first):
        m_cur = blocks[0]
        for blk in blocks[1:]:
            m_cur = jnp.maximum(m_cur, blk)
        m_new = jnp.max(m_cur, axis=-1, keepdims=True)
        if not first:
            m_prev = m_ref[...]
            m_new = jnp.maximum(m_prev, m_new)
        pr = [jnp.exp2(blk - m_new) for blk in blocks]
        l_add = pr[0]
        for e in pr[1:]:
            l_add = l_add + e
        l_add = jnp.sum(l_add, axis=-1, keepdims=True)
        per_v = len(blocks) // len(v_list)
        pv = None
        for jj, vv in enumerate(v_list):
            part = jnp.dot(jnp.concatenate(pr[jj * per_v:(jj + 1) * per_v], axis=1), vv, preferred_element_type=F32)
            pv = part if pv is None else pv + part
        if first:
            l_ref[...] = l_add
            acc_ref[...] = pv
        else:
            alpha = jnp.exp2(m_prev - m_new)
            l_ref[...] = alpha * l_ref[...] + l_add
            acc_ref[...] = alpha * acc_ref[...] + pv
        m_ref[...] = m_new

    @pl.when(g == 0)
    def _new_tokens():
        for h in range(n_heads):
            q = q_ref[0, :, h * dv:(h + 1) * dv].astype(F32)
            q_up = jnp.where((row_q < t_new) & (lane_q < dh), q, 0.0)
            q_dn = jnp.where((row_q >= t_new) & (lane_q >= dh), pltpu.roll(q, t_new, 0), 0.0)
            qall_ref[h * rows:(h + 1) * rows, :] = q_up + q_dn
        pad = jnp.zeros((LANE - rows * n_heads, dv), F32)
        kn = jnp.concatenate([kn_ref[0].reshape(rows * n_heads, dv), pad], axis=0)
        vn = jnp.concatenate([vn_ref[0].reshape(rows * n_heads, dv), pad], axis=0)
        r = lax.broadcasted_iota(I32, (n_heads * rows, LANE), 0) & (rows - 1)
        tok_q = jnp.where(r >= t_new, r - t_new, r)
        tok_k = lax.broadcasted_iota(I32, (n_heads * rows, LANE), 1) >> (n_heads.bit_length() - 1)
        accumulate(scores(kn, own_head & (tok_k <= tok_q)), [vn], True)

    blocks = []
    for kr in k_refs:
        blocks += scores(kr[...].reshape(kv_rows, dv), own_head)
    accumulate(blocks, [vr[...].reshape(kv_rows, dv) for vr in v_refs], False)

    @pl.when(g == pl.num_programs(1) - 1)
    def _finish():
        lam = _lambda(lq1_ref, lk1_ref, lq2_ref, lk2_ref, lam_init)
        a_all = acc_ref[...] / l_ref[...]
        for h in range(n_heads):
            a = a_all[h * rows:(h + 1) * rows]
            o = a - lam * pltpu.roll(a, t_new, 0)
            o = _rms(o, gsub_ref[...]) * (1.0 - lam_init)
            o_ref[0, :, h * dv:(h + 1) * dv] = jnp.where(row_q < t_new, o, 0.0)


def _diff_attention_decode(q, k_new, v_new, cache_k, cache_v, layer, page_table, lams, g_sub, lam_init, t_new):
    batch, rows, width = q.shape
    assert rows == 2 * t_new == SUBLANE, "decode kernel stacks the two softmax maps of the new tokens in one sublane tile"
    dv = width // N_DIFF_HEADS
    dh = dv // 2
    n_pages = page_table.shape[1]
    page = cache_k.shape[2]
    n_pg = next(c for c in (16, 8, 4, 2, 1) if n_pages % c == 0)
    steps = n_pages // n_pg
    seq_blk = lambda b, g, pt: (b, 0, 0)
    seq_blk4 = lambda b, g, pt: (b, 0, 0, 0)
    fixed = lambda b, g, pt: (0, 0)

    def page_spec(jj):
        return pl.BlockSpec((None, None, page, N_DIFF_HEADS, dv),
                            lambda b, g, pt: (layer, pt[b * n_pages + g * n_pg + jj], 0, 0, 0))

    grid_spec = pltpu.PrefetchScalarGridSpec(
        num_scalar_prefetch=1,
        grid=(batch, steps),
        in_specs=[pl.BlockSpec((1, rows, width), seq_blk)]
                 + [pl.BlockSpec((1, rows, N_DIFF_HEADS, dv), seq_blk4)] * 2
                 + [pl.BlockSpec((1, dh), fixed)] * 4 + [pl.BlockSpec((1, dv), fixed)]
                 + [page_spec(jj) for jj in range(n_pg)] * 2,
        out_specs=pl.BlockSpec((1, rows, width), seq_blk),
        scratch_shapes=[pltpu.VMEM((N_DIFF_HEADS * rows, 1), F32), pltpu.VMEM((N_DIFF_HEADS * rows, 1), F32),
                        pltpu.VMEM((N_DIFF_HEADS * rows, dv), F32), pltpu.VMEM((N_DIFF_HEADS * rows, dv), F32)],
    )
    return pl.pallas_call(
        functools.partial(_diff_decode_kernel, n_pg=n_pg, t_new=t_new, dh=dh, lam_init=lam_init),
        grid_spec=grid_spec,
        out_shape=jax.ShapeDtypeStruct((batch, rows, width), F32),
        compiler_params=_cparams(("parallel", "arbitrary"), 48),
        name="diff_attention_decode",
    )(page_table.reshape(-1), q, k_new, v_new, *lams, g_sub, *([cache_k] * n_pg), *([cache_v] * n_pg))


def _gla_chunk(q, k, v, g, s_t, chunk):
    nb = chunk // GLA_SUB
    row1 = lax.broadcasted_iota(I32, (chunk, 1), 0)
    b = g
    step = 1
    while step < chunk:
        b = b + jnp.where(row1 >= step, pltpu.roll(b, step, 0), 0.0)
        step *= 2
    b_last = b[chunk - 1:chunk, :]

    qe = (q * jnp.exp(b)).astype(BF16)
    o = lax.dot_general(qe, s_t.astype(BF16), _NT, preferred_element_type=F32)
    kd = (k * jnp.exp(b_last - b)).astype(BF16)
    s_new = s_t * jnp.exp(b_last) + lax.dot_general(v.astype(BF16), kd, _TN, preferred_element_type=F32)

    row = lax.broadcasted_iota(I32, (chunk, chunk), 0)
    col = lax.broadcasted_iota(I32, (chunk, chunk), 1)
    row_in_blk = row1 & (GLA_SUB - 1)

    att = jnp.where(col == row, jnp.sum(q * k, axis=-1, keepdims=True), 0.0)
    for d in range(1, min(GLA_SUB, chunk)):
        k_sh = pltpu.roll(k, d, 0)
        b_sh = pltpu.roll(b, d, 0)
        decay = jnp.exp(jnp.where(row_in_blk >= d, b - b_sh, -jnp.inf))
        w = jnp.sum(q * k_sh * decay, axis=-1, keepdims=True)
        att = att + jnp.where(col == row - d, w, 0.0)

    dk = q.shape[1]
    for j in range(nb - 1):
        r0 = (j + 1) * GLA_SUB
        b_end = b[r0 - 1:r0, :]
        kj = k[r0 - GLA_SUB:r0] * jnp.exp(b_end - b[r0 - GLA_SUB:r0])
        above = [jnp.zeros((r0 - GLA_SUB, dk), F32)] if j else []
        kj = jnp.concatenate(above + [kj, jnp.zeros((chunk - r0, dk), F32)], axis=0)
        qj = (q[r0:] * jnp.exp(b[r0:] - b_end)).astype(BF16)
        a = lax.dot_general(qj, kj.astype(BF16), _NT, preferred_element_type=F32)
        att = att + jnp.concatenate([jnp.zeros((r0, chunk), F32), a], axis=0)

    o = o + jnp.dot(att.astype(BF16), v.astype(BF16), preferred_element_type=F32)
    return o, s_new


def _gla_kernel(gqk_ref, gv_ref, gr_ref, glog_ref, s0_ref, ggla_ref, o_ref, s_ref, st_ref,
                *, chunk, t_valid, dk, dv, scale):
    tb = pl.program_id(1)
    t_blk = gqk_ref.shape[1]
    n_heads = N_GLA_HEADS

    @pl.when(tb == 0)
    def _load_state():
        for h in range(n_heads):
            st_ref[h] = s0_ref[0, h].T

    g_gla = ggla_ref[...]

    def body(c, carry):
        r0 = pl.multiple_of(c * chunk, chunk)
        rows = pl.ds(r0, chunk)
        pos = tb * t_blk + r0 + lax.broadcasted_iota(I32, (chunk, 1), 0)
        live = pos < t_valid
        for h in range(n_heads):
            q = gqk_ref[0, rows, h * dk:(h + 1) * dk] * scale
            k = gqk_ref[0, rows, (n_heads + h) * dk:(n_heads + h + 1) * dk]
            v = gv_ref[0, rows, h * dv:(h + 1) * dv]
            g = jnp.where(live, glog_ref[0, rows, h * dk:(h + 1) * dk], 0.0)
            k = jnp.where(live, k, 0.0)
            o, s_new = _gla_chunk(q, k, v, g, st_ref[h], chunk)
            st_ref[h] = s_new
            gr = gr_ref[0, rows, h * dv:(h + 1) * dv]
            o_ref[0, rows, h * dv:(h + 1) * dv] = (_rms(o, g_gla) * (gr * _sigmoid(gr))).astype(o_ref.dtype)
        return carry

    lax.fori_loop(0, t_blk // chunk, body, 0)

    @pl.when(tb == pl.num_programs(1) - 1)
    def _store_state():
        for h in range(n_heads):
            s_ref[0, h] = st_ref[h].T


def _gla(gqk, gv, gr, glog, s0, g_gla, t_valid, out_dtype):
    batch, seq, _ = gqk.shape
    dk, dv = s0.shape[2], s0.shape[3]
    chunk = next(c for c in (64, 32, 16, 8) if seq % c == 0)
    t_blk = _tile(seq, 256)
    tok = lambda b, t: (b, t, 0)
    return pl.pallas_call(
        functools.partial(_gla_kernel, chunk=chunk, t_valid=t_valid, dk=dk, dv=dv, scale=dk ** -0.5),
        grid=(batch, seq // t_blk),
        in_specs=[
            pl.BlockSpec((1, t_blk, gqk.shape[2]), tok),
            pl.BlockSpec((1, t_blk, gv.shape[2]), tok),
            pl.BlockSpec((1, t_blk, gr.shape[2]), tok),
            pl.BlockSpec((1, t_blk, glog.shape[2]), tok),
            pl.BlockSpec((1, N_GLA_HEADS, dk, dv), lambda b, t: (b, 0, 0, 0)),
            pl.BlockSpec((1, dv), lambda b, t: (0, 0)),
        ],
        out_specs=[
            pl.BlockSpec((1, t_blk, gv.shape[2]), tok),
            pl.BlockSpec((1, N_GLA_HEADS, dk, dv), lambda b, t: (b, 0, 0, 0)),
        ],
        out_shape=[jax.ShapeDtypeStruct(gv.shape, out_dtype), jax.ShapeDtypeStruct(s0.shape, F32)],
        scratch_shapes=[pltpu.VMEM((N_GLA_HEADS, dv, dk), F32)],
        compiler_params=_cparams(("parallel", "arbitrary"), 32),
        name="gla",
    )(gqk, gv, gr, glog, s0, g_gla)


def _cross_kernel(q_ref, mk_ref, mv_ref, o_ref, kh_ref, vh_ref, *, dh):
    @pl.when(pl.program_id(1) == 0)
    def _split_heads():
        for h in range(N_MEM_HEADS):
            kh_ref[h] = mk_ref[0, :, h, :].astype(BF16)
            vh_ref[h] = mv_ref[0, :, h, :].astype(BF16)

    for h in range(N_MEM_HEADS):
        sl = slice(h * dh, (h + 1) * dh)
        s = lax.dot_general(q_ref[0, :, sl].astype(BF16), kh_ref[h], _NT, preferred_element_type=F32)
        pr = jnp.exp(s - jnp.max(s, axis=-1, keepdims=True))
        pr = pr / jnp.sum(pr, axis=-1, keepdims=True)
        o_ref[0, :, sl] = jnp.dot(pr.astype(BF16), vh_ref[h], preferred_element_type=F32).astype(o_ref.dtype)


def _cross_attention(q, mk, mv, mem_index):
    batch, seq, d = q.shape
    mem, n_heads, dh = mk.shape[-3:]
    tq = _tile(seq, 512)
    lead = mk.ndim - 4
    mem_spec = pl.BlockSpec((None,) * lead + (1, mem, n_heads, dh), lambda b, t: mem_index(b) + (0, 0, 0))
    return pl.pallas_call(
        functools.partial(_cross_kernel, dh=dh),
        grid=(batch, seq // tq),
        in_specs=[pl.BlockSpec((1, tq, d), lambda b, t: (b, t, 0)), mem_spec, mem_spec],
        out_specs=pl.BlockSpec((1, tq, d), lambda b, t: (b, t, 0)),
        out_shape=jax.ShapeDtypeStruct(q.shape, BF16 if tq % 16 == 0 else F32),
        scratch_shapes=[pltpu.VMEM((n_heads, mem, dh), BF16), pltpu.VMEM((n_heads, mem, dh), BF16)],
        compiler_params=_cparams(("parallel", "arbitrary"), 40),
        name="cross_attention",
    )(q, mk, mv)


def _two_group_specs(n_a, n_b, tm, d):
    ta = n_a // tm
    return (pl.BlockSpec((tm, d), lambda i, *_: (jnp.minimum(i, ta - 1), 0)),
            pl.BlockSpec((tm, d), lambda i, *_: (jnp.maximum(i - ta, 0), 0)))


def _route_kernel(xa_ref, xb_ref, g_ref, wh_ref, wl_ref, xn_ref, idx_ref, wgt_ref, *, tiles_a):
    x = jnp.where(pl.program_id(0) < tiles_a, xa_ref[...], xb_ref[...])
    xn = _rms(x, g_ref[...])
    xn_ref[...] = xn
    xh = xn.astype(BF16)
    xl = (xn - xh.astype(F32)).astype(BF16)
    logits = (jnp.dot(xh, wh_ref[...], preferred_element_type=F32)
              + jnp.dot(xl, wh_ref[...], preferred_element_type=F32)
              + jnp.dot(xh, wl_ref[...], preferred_element_type=F32))
    lane = lax.broadcasted_iota(I32, logits.shape, 1)
    lane_f = lane.astype(F32)
    neg = -jnp.inf

    def first_max(vals):
        top = jnp.max(vals, axis=-1, keepdims=True)
        first = jnp.min(jnp.where(vals == top, lane_f, float(LANE)), axis=-1, keepdims=True)
        return top, first.astype(I32)

    is_grp = lane < N_GROUPS
    g_top, g_idx = first_max(jnp.where(is_grp, logits, neg))
    g_w = 1.0 / jnp.sum(jnp.where(is_grp, jnp.exp(logits - g_top), 0.0), axis=-1, keepdims=True)
    lo = N_GROUPS + g_idx * EXPERTS_PER_GROUP
    le = jnp.where((lane >= lo) & (lane < lo + EXPERTS_PER_GROUP), logits, neg)
    v1, i1 = first_max(le)
    v2, i2 = first_max(jnp.where(lane == i1, neg, le))
    e2 = jnp.exp(v2 - v1)
    w1 = g_w / (1.0 + e2)
    w2 = g_w * e2 / (1.0 + e2)
    idx_ref[...] = jnp.where(lane == 0, i1 - N_GROUPS, jnp.where(lane == 1, i2 - N_GROUPS, 0))
    wgt_ref[...] = jnp.where(lane == 0, w1, jnp.where(lane == 1, w2, 0.0))


def _route(xa, xb, tm, g_ffn, w_hi, w_lo):
    d = xa.shape[1]
    n = xa.shape[0] + xb.shape[0]
    row = lambda i: (i, 0)
    fixed = lambda i: (0, 0)
    return pl.pallas_call(
        functools.partial(_route_kernel, tiles_a=xa.shape[0] // tm),
        grid=(n // tm,),
        in_specs=[*_two_group_specs(xa.shape[0], xb.shape[0], tm, d), pl.BlockSpec((1, d), fixed),
                  pl.BlockSpec((d, LANE), fixed), pl.BlockSpec((d, LANE), fixed)],
        out_specs=[pl.BlockSpec((tm, d), row), pl.BlockSpec((tm, LANE), row), pl.BlockSpec((tm, LANE), row)],
        out_shape=[jax.ShapeDtypeStruct((n, d), F32), jax.ShapeDtypeStruct((n, LANE), I32),
                   jax.ShapeDtypeStruct((n, LANE), F32)],
        compiler_params=_cparams(("parallel",), 40),
        name="moe_route",
    )(xa, xb, g_ffn, w_hi, w_lo)


def _row_copies(n_rows, make_copy):
    def start(it, c):
        for u in range(DMA_UNROLL):
            for kk in range(2):
                make_copy(it * DMA_UNROLL + u, kk).start(priority=kk)
        return c

    def wait(it, c):
        for u in range(DMA_UNROLL):
            for kk in range(2):
                make_copy(it * DMA_UNROLL + u, kk).wait()
        return c

    assert n_rows % DMA_UNROLL == 0
    lax.fori_loop(0, n_rows // DMA_UNROLL, start, 0)
    lax.fori_loop(0, n_rows // DMA_UNROLL, wait, 0)


def _dispatch_kernel(pos_ref, pend_ref, x_ref, xs_ref, zero_ref, sem, zsem, *, tile_e):
    i = pl.program_id(0)
    tm = x_ref.shape[0]

    @pl.when(i == 0)
    def _zero_padding():
        zero_ref[...] = jnp.zeros(zero_ref.shape, F32)

        def fill(e):
            start = pl.multiple_of(jnp.maximum(pend_ref[e] - tile_e, 0), tile_e)
            return pltpu.make_async_copy(zero_ref, xs_ref.at[pl.ds(start, tile_e), :], zsem)

        for e in range(N_EXPERTS):
            fill(e).start()
        for e in range(N_EXPERTS):
            fill(e).wait()

        def tail(t):
            return pltpu.make_async_copy(zero_ref, xs_ref.at[pl.ds(pl.multiple_of(t * tile_e, tile_e), tile_e), :], zsem)

        first_unused = pend_ref[N_EXPERTS - 1] // tile_e
        n_tiles = xs_ref.shape[0] // tile_e
        lax.fori_loop(first_unused, n_tiles, lambda t, c: (tail(t).start(), c)[1], 0)
        lax.fori_loop(first_unused, n_tiles, lambda t, c: (tail(t).wait(), c)[1], 0)

    def row_copy(r, kk):
        dst = pos_ref[(i * tm + r) * 2 + kk]
        return pltpu.make_async_copy(x_ref.at[pl.ds(r, 1), :], xs_ref.at[pl.ds(dst, 1), :], sem.at[kk])

    _row_copies(tm, row_copy)


def _dispatch(xn, tm, pos, p_end, n_slots, tile_e):
    n, d = xn.shape
    grid_spec = pltpu.PrefetchScalarGridSpec(
        num_scalar_prefetch=2,
        grid=(n // tm,),
        in_specs=[pl.BlockSpec((tm, d), lambda i, pos, pend: (i, 0))],
        out_specs=pl.BlockSpec(memory_space=pl.ANY),
        scratch_shapes=[pltpu.VMEM((tile_e, d), F32), pltpu.SemaphoreType.DMA((2,)), pltpu.SemaphoreType.DMA(())],
    )
    return pl.pallas_call(
        functools.partial(_dispatch_kernel, tile_e=tile_e),
        grid_spec=grid_spec,
        out_shape=jax.ShapeDtypeStruct((n_slots, d), F32),
        compiler_params=_cparams(("arbitrary",), 32),
        name="moe_dispatch",
    )(pos, p_end, xn)


def _experts_kernel(te_ref, nu_ref, x_ref, wg_ref, wu_ref, wd_ref, y_ref, wgb_ref, wub_ref, wdb_ref):
    t = pl.program_id(0)
    prev = te_ref[jnp.maximum(t - 1, 0)]

    @pl.when((t == 0) | (te_ref[t] != prev))
    def _cast_weights():
        wgb_ref[...] = wg_ref[...].astype(BF16)
        wub_ref[...] = wu_ref[...].astype(BF16)
        wdb_ref[...] = wd_ref[...].astype(BF16)

    @pl.when(t < nu_ref[0])
    def _mlp():
        x = x_ref[...].astype(BF16)
        hg = jnp.dot(x, wgb_ref[...], preferred_element_type=F32)
        hu = jnp.dot(x, wub_ref[...], preferred_element_type=F32)
        h = (hg * _sigmoid(hg)) * hu
        y_ref[...] = jnp.dot(h.astype(BF16), wdb_ref[...], preferred_element_type=F32)

    @pl.when(t >= nu_ref[0])
    def _unused():
        y_ref[...] = jnp.zeros(y_ref.shape, F32)


def _experts(xs, tile_expert, n_used, w_gate, w_up, w_down, tm):
    n_slots, d = xs.shape
    de = w_gate.shape[-1]
    lead = w_gate.ndim - 3
    layer_idx = w_gate.shape[:lead]
    assert all(s == 1 for s in layer_idx)
    wspec = lambda r, c: pl.BlockSpec((None,) * (lead + 1) + (r, c), lambda t, te, nu: (0,) * lead + (te[t], 0, 0))
    grid_spec = pltpu.PrefetchScalarGridSpec(
        num_scalar_prefetch=2,
        grid=(n_slots // tm,),
        in_specs=[
            pl.BlockSpec((tm, d), lambda t, te, nu: (jnp.minimum(t, nu[0] - 1), 0)),
            wspec(d, de), wspec(d, de), wspec(de, d),
        ],
        out_specs=pl.BlockSpec((tm, d), lambda t, te, nu: (t, 0)),
        scratch_shapes=[pltpu.VMEM((d, de), BF16), pltpu.VMEM((d, de), BF16), pltpu.VMEM((de, d), BF16)],
    )
    return pl.pallas_call(
        _experts_kernel,
        grid_spec=grid_spec,
        out_shape=jax.ShapeDtypeStruct((n_slots, d), F32),
        compiler_params=_cparams(("arbitrary",), 52),
        name="moe_experts",
    )(tile_expert, n_used, xs, w_gate, w_up, w_down)


def _combine_kernel(pos_ref, xa_ref, xb_ref, wgt_ref, g_ref, ys_ref, oa_ref, ob_ref, buf_ref, sem, *, tiles_a):
    i = pl.program_id(0)
    tm = xa_ref.shape[0]

    def row_copy(r, kk):
        src = pos_ref[(i * tm + r) * 2 + kk]
        return pltpu.make_async_copy(ys_ref.at[pl.ds(src, 1), :], buf_ref.at[kk, pl.ds(r, 1), :], sem.at[kk])

    _row_copies(tm, row_copy)
    wgt = wgt_ref[...]
    first_group = i < tiles_a
    x = jnp.where(first_group, xa_ref[...], xb_ref[...])
    y = _rms(x + (wgt[:, 0:1] * buf_ref[0] + wgt[:, 1:2] * buf_ref[1]), g_ref[...])

    @pl.when(first_group)
    def _():
        oa_ref[...] = y

    @pl.when(jnp.logical_not(first_group))
    def _():
        ob_ref[...] = y


def _combine(xa, xb, tm, wgt, g_final, ys, pos):
    d = xa.shape[1]
    n = xa.shape[0] + xb.shape[0]
    grid_spec = pltpu.PrefetchScalarGridSpec(
        num_scalar_prefetch=1,
        grid=(n // tm,),
        in_specs=[*_two_group_specs(xa.shape[0], xb.shape[0], tm, d), pl.BlockSpec((tm, LANE), lambda i, pos: (i, 0)),
                  pl.BlockSpec((1, d), lambda i, pos: (0, 0)), pl.BlockSpec(memory_space=pl.ANY)],
        out_specs=list(_two_group_specs(xa.shape[0], xb.shape[0], tm, d)),
        scratch_shapes=[pltpu.VMEM((2, tm, d), F32), pltpu.SemaphoreType.DMA((2,))],
    )
    return pl.pallas_call(
        functools.partial(_combine_kernel, tiles_a=xa.shape[0] // tm),
        grid_spec=grid_spec,
        out_shape=[jax.ShapeDtypeStruct(xa.shape, F32), jax.ShapeDtypeStruct(xb.shape, F32)],
        compiler_params=_cparams(("arbitrary",), 32),
        name="moe_combine",
    )(pos, xa, xb, wgt, g_final, ys)


def _moe_and_final_norm(xa, xb, g_ffn, w_route_hi, w_route_lo, w_gate, w_up, w_down, g_final):
    d = xa.shape[1]
    n = xa.shape[0] + xb.shape[0]
    tm_tok = _tile(math.gcd(xa.shape[0], xb.shape[0]), 256)
    assert tm_tok % DMA_UNROLL == 0
    xn, ridx, rw = _route(xa, xb, tm_tok, g_ffn, w_route_hi, w_route_lo)
    tm = 256 if n >= 2048 else 128
    e = ridx[:, :2].reshape(-1)
    onehot = (e[:, None] == jnp.arange(N_EXPERTS, dtype=I32)[None, :]).astype(I32)
    csum = jnp.cumsum(onehot, axis=0)
    counts = csum[-1]
    padded = (counts + tm - 1) // tm * tm
    p_end = jnp.cumsum(padded).astype(I32)
    pos = jnp.sum(onehot * ((p_end - padded)[None, :] + csum - 1), axis=1).astype(I32)
    n_tiles = (2 * n + N_EXPERTS * (tm - 1)) // tm + 1
    tile_start = jnp.arange(n_tiles, dtype=I32) * tm
    tile_expert = jnp.minimum(jnp.sum((tile_start[:, None] >= p_end[None, :]).astype(I32), axis=1), N_EXPERTS - 1)
    n_used = (p_end[-1:] // tm).astype(I32)
    xs = _dispatch(xn, tm_tok, pos, p_end, n_tiles * tm, tm)
    ys = _experts(xs, tile_expert.astype(I32), n_used, w_gate, w_up, w_down, tm)
    return _combine(xa, xb, tm_tok, rw, g_final, ys, pos)


def _layer(x3, s0, t_valid, mk, mv, mem_index, diff_fn, lw):
    batch, seq, d = x3.shape
    n = batch * seq
    x = x3.reshape(n, d)
    qs, k4, v4, kb, vb, gqk, gv, gr, glog = _in_projection(x, lw["g_mix"], lw["w_in_main"], lw["w_in_ga"],
                                                            lw["w_a2p"], lw["b_a"])
    d_out = diff_fn(qs, k4, v4, kb, vb)
    r3 = lambda a: a.reshape(batch, seq, a.shape[-1])
    g_out, s_new = _gla(r3(gqk), r3(gv), r3(gr), r3(glog), s0, lw["g_gla"], t_valid,
                        BF16 if seq % 16 == 0 else F32)
    x1 = _matmul_residual([d_out.reshape(n, -1), g_out.reshape(n, -1)], lw["w_o"], x, "mixer_out")
    dh_mem = d // N_MEM_HEADS
    xq = _norm_matmul(x1, lw["g_cross"], lw["w_q_mem"], BF16 if seq % 16 == 0 else F32, "cross_q",
                      out_scale=dh_mem ** -0.5)
    ca = _cross_attention(xq.reshape(batch, seq, d), mk, mv, mem_index)
    x2 = _matmul_residual([ca.reshape(n, d)], lw["w_o_mem"], x1, "cross_out")
    return x2, k4, v4, s_new


def kernel(x_prompt, x_sample, mem_prompt, cache_k, cache_v, state_gla, cache_mem_k, cache_mem_v, page_table, g_mix, w_in, w_a2, b_a, lambda_q1, lambda_k1, lambda_q2, lambda_k2, g_subln, g_gla, w_o, g_mem, g_cross, w_q_mem, w_k_mem, w_v_mem, w_o_mem, g_ffn, w_group, w_router, w_gate, w_up, w_down, g_final):
    depth = w_in.shape[0]
    batch, seq, d = x_prompt.shape
    dec_batch, dec_seq, _ = x_sample.shape
    mem_len = mem_prompt.shape[1]
    width = d // 2
    main_cols = w_in.shape[2] - GATE_RANK
    assert main_cols == 6 * width and w_a2.shape[1:] == (GATE_RANK, width // 2)
    dk_gla = width // N_GLA_HEADS // 2
    dv_gla = width // N_GLA_HEADS
    dv_diff = width // N_DIFF_HEADS
    dec_pad = -(-dec_seq // SUBLANE) * SUBLANE

    xp = x_prompt
    xs = jnp.pad(x_sample, ((0, 0), (0, dec_pad - dec_seq), (0, 0)))
    s0_prompt = jnp.zeros((batch, N_GLA_HEADS, dk_gla, dv_gla), F32)
    row2 = lambda a: a.reshape(1, -1)
    outs = {name: [] for name in ("kp", "vp", "sp", "mkp", "mvp", "ks", "vs", "ss")}
    for l in range(depth):
        lam_init = 0.8 - 0.6 * math.exp(-0.3 * l)
        lw = dict(
            g_mix=row2(g_mix[l]), g_gla=row2(g_gla[l]), g_cross=row2(g_cross[l]), b_a=row2(b_a[l]),
            w_in_main=w_in[l][:, :main_cols].astype(BF16),
            w_in_ga=jnp.pad(w_in[l][:, main_cols:], ((0, 0), (0, LANE - GATE_RANK))).astype(BF16),
            w_a2p=jnp.pad(w_a2[l], ((0, LANE - GATE_RANK), (0, 0))).astype(BF16),
            w_o=w_o[l].astype(BF16), w_q_mem=w_q_mem[l].astype(BF16), w_o_mem=w_o_mem[l].astype(BF16),
        )
        lams = (row2(lambda_q1[l]), row2(lambda_k1[l]), row2(lambda_q2[l]), row2(lambda_k2[l]))
        g_sub = row2(g_subln[l])
        w_route = jnp.pad(jnp.concatenate([w_group[l], w_router[l]], axis=1),
                          ((0, 0), (0, LANE - N_GROUPS - N_EXPERTS)))
        w_route_hi = w_route.astype(BF16)
        w_route_lo = (w_route - w_route_hi.astype(F32)).astype(BF16)

        mem2 = mem_prompt.reshape(batch * mem_len, d)
        mkp = _norm_matmul(mem2, row2(g_mem[l]), w_k_mem[l].astype(BF16), F32, "mem_k", n_heads=N_MEM_HEADS)
        mvp = _norm_matmul(mem2, row2(g_mem[l]), w_v_mem[l].astype(BF16), F32, "mem_v", n_heads=N_MEM_HEADS)
        mkp = mkp.reshape(batch, mem_len, N_MEM_HEADS, -1)
        mvp = mvp.reshape(batch, mem_len, N_MEM_HEADS, -1)

        prompt_attn = lambda qs, k4, v4, kb, vb: _diff_attention_prompt(qs, kb, vb, lams, g_sub, lam_init, batch, seq,
                                                                        BF16)
        xp2, kp, vp, sp = _layer(xp, s0_prompt, seq, mkp, mvp, lambda b: (b,), prompt_attn, lw)

        sample_attn = lambda qs, k4, v4, kb, vb, l=l: _diff_attention_decode(
            qs.reshape(dec_batch, dec_pad, width), k4.reshape(dec_batch, dec_pad, N_DIFF_HEADS, dv_diff),
            v4.reshape(dec_batch, dec_pad, N_DIFF_HEADS, dv_diff), cache_k, cache_v, l, page_table, lams, g_sub,
            lam_init, dec_seq)
        xs2, ks, vs, ss = _layer(xs, state_gla[l], dec_seq, cache_mem_k, cache_mem_v, lambda b, l=l: (l, b),
                                 sample_attn, lw)

        assert l == depth - 1, "the MoE combine kernel fuses the final norm; deeper stacks need an un-normed variant"
        moe = functools.partial(_moe_and_final_norm, g_ffn=row2(g_ffn[l]), w_route_hi=w_route_hi,
                                w_route_lo=w_route_lo, w_gate=w_gate[l:l + 1], w_up=w_up[l:l + 1],
                                w_down=w_down[l:l + 1], g_final=row2(g_final))
        xp, xs = moe(xp2, xs2)
        xp = xp.reshape(batch, seq, d)
        xs = xs.reshape(dec_batch, dec_pad, d)

        outs["kp"].append(kp.reshape(batch, seq, N_DIFF_HEADS, dv_diff))
        outs["vp"].append(vp.reshape(batch, seq, N_DIFF_HEADS, dv_diff))
        outs["sp"].append(sp)
        outs["mkp"].append(mkp)
        outs["mvp"].append(mvp)
        outs["ks"].append(ks.reshape(dec_batch, dec_pad, N_DIFF_HEADS, dv_diff)[:, :dec_seq])
        outs["vs"].append(vs.reshape(dec_batch, dec_pad, N_DIFF_HEADS, dv_diff)[:, :dec_seq])
        outs["ss"].append(ss)

    st = lambda name: jnp.stack(outs[name])
    return (xp, xs[:, :dec_seq], st("kp"), st("vp"), st("sp"), st("mkp"), st("mvp"), st("ks"), st("vs"), st("ss"))
```

```python
import functools
import math

import jax
import jax.numpy as jnp
from jax import lax
from jax.experimental import pallas as pl
from jax.experimental.pallas import tpu as pltpu

F32 = jnp.float32
BF16 = jnp.bfloat16
I32 = jnp.int32

N_DIFF_HEADS = 4
N_GLA_HEADS = 4
GATE_RANK = 16
GATE_TAU = 16.0
N_MEM_HEADS = 4
N_GROUPS = 4
EXPERTS_PER_GROUP = 8
N_EXPERTS = N_GROUPS * EXPERTS_PER_GROUP
RMS_EPS = 1e-6
GLA_SUB = 8
LOG2E = math.log2(math.e)

LANE = 128
SUBLANE = 8
MIB = 1024 * 1024
DMA_UNROLL = 8

_NT = (((1,), (1,)), ((), ()))
_TN = (((0,), (0,)), ((), ()))


def _cparams(semantics, vmem_mib):
    return pltpu.CompilerParams(dimension_semantics=semantics, vmem_limit_bytes=int(vmem_mib * MIB))


def _tile(n, target):
    if n <= target:
        return n
    t = target - target % SUBLANE
    while t >= SUBLANE:
        if n % t == 0:
            return t
        t -= SUBLANE
    raise ValueError(f"no sublane-aligned tile of {n} below {target}")


def _rms(x, g):
    ms = jnp.mean(x * x, axis=-1, keepdims=True)
    return x * lax.rsqrt(ms + RMS_EPS) * g


def _sigmoid(x):
    return 1.0 / (1.0 + jnp.exp(-x))


def _resident(a):
    return pl.BlockSpec(a.shape, lambda *_: (0,) * a.ndim, pipeline_mode=pl.Buffered(1))


def _store_heads(o_ref, val, n_heads):
    w = val.shape[1] // n_heads
    for h in range(n_heads):
        o_ref[:, h, :] = val[:, h * w:(h + 1) * w].astype(o_ref.dtype)


def _inproj_kernel(x_ref, g_ref, w_ref, wga_ref, wa2_ref, ba_ref,
                   qs_ref, k_ref, v_ref, kb_ref, vb_ref, gqk_ref, gv_ref, gr_ref, glog_ref, *, q_scale):
    xn = _rms(x_ref[...], g_ref[...]).astype(BF16)
    wblk = qs_ref.shape[1]

    def proj(group):
        return jnp.dot(xn, w_ref[:, group * wblk:(group + 1) * wblk], preferred_element_type=F32)

    qs_ref[...] = (proj(0) * q_scale).astype(BF16)
    for group, (o_ref, ob_ref) in ((1, (k_ref, kb_ref)), (2, (v_ref, vb_ref))):
        r = proj(group)
        _store_heads(o_ref, r, N_DIFF_HEADS)
        ob_ref[...] = r.astype(BF16)
    for group, o_ref in ((3, gqk_ref), (4, gv_ref), (5, gr_ref)):
        o_ref[...] = proj(group)
    ga = jnp.dot(xn, wga_ref[...], preferred_element_type=F32)
    z = jnp.dot(ga.astype(BF16), wa2_ref[...], preferred_element_type=F32) + ba_ref[...]
    glog_ref[...] = (jnp.minimum(z, 0.0) - jnp.log1p(jnp.exp(-jnp.abs(z)))) * (1.0 / GATE_TAU)


def _in_projection(x, g_mix, w_main, w_ga, w_a2p, b_a):
    n, d = x.shape
    wblk = d // 2
    assert w_main.shape[1] == 6 * wblk
    dv = wblk // N_DIFF_HEADS
    gk = w_a2p.shape[1]
    tm = _tile(n, 256)
    row = lambda i: (i, 0)
    row3 = lambda i: (i, 0, 0)
    flat = lambda dt: (jax.ShapeDtypeStruct((n, wblk), dt), pl.BlockSpec((tm, wblk), row))
    heads = (jax.ShapeDtypeStruct((n, N_DIFF_HEADS, dv), F32), pl.BlockSpec((tm, N_DIFF_HEADS, dv), row3))
    outs = [flat(BF16), heads, heads, flat(BF16), flat(BF16), flat(F32), flat(F32), flat(F32),
            (jax.ShapeDtypeStruct((n, gk), F32), pl.BlockSpec((tm, gk), row))]
    return pl.pallas_call(
        functools.partial(_inproj_kernel, q_scale=(dv // 2) ** -0.5 * LOG2E),
        grid=(n // tm,),
        in_specs=[pl.BlockSpec((tm, d), row), _resident(g_mix), _resident(w_main), _resident(w_ga),
                  _resident(w_a2p), _resident(b_a)],
        out_specs=[o[1] for o in outs],
        out_shape=[o[0] for o in outs],
        compiler_params=_cparams(("parallel",), 56),
        name="in_projection",
    )(x, g_mix, w_main, w_ga, w_a2p, b_a)


def _norm_mm_kernel(x_ref, g_ref, w_ref, o_ref, *, out_scale, n_heads):
    xn = _rms(x_ref[...], g_ref[...]).astype(BF16)
    r = jnp.dot(xn, w_ref[...], preferred_element_type=F32)
    if out_scale != 1.0:
        r = r * out_scale
    if n_heads:
        _store_heads(o_ref, r, n_heads)
    else:
        o_ref[...] = r.astype(o_ref.dtype)


def _norm_matmul(x, g, w, out_dtype, name, out_scale=1.0, n_heads=0):
    n, d = x.shape
    dout = w.shape[1]
    tm = _tile(n, 512)
    if n_heads:
        out_spec = pl.BlockSpec((tm, n_heads, dout // n_heads), lambda i: (i, 0, 0))
        out_shape = jax.ShapeDtypeStruct((n, n_heads, dout // n_heads), out_dtype)
    else:
        out_spec = pl.BlockSpec((tm, dout), lambda i: (i, 0))
        out_shape = jax.ShapeDtypeStruct((n, dout), out_dtype)
    return pl.pallas_call(
        functools.partial(_norm_mm_kernel, out_scale=out_scale, n_heads=n_heads),
        grid=(n // tm,),
        in_specs=[pl.BlockSpec((tm, d), lambda i: (i, 0)), _resident(g), _resident(w)],
        out_specs=out_spec,
        out_shape=out_shape,
        compiler_params=_cparams(("parallel",), 48),
        name=name,
    )(x, g, w)


def _mm_res_kernel(*refs, widths):
    a_refs = refs[:len(widths)]
    w_ref, res_ref, o_ref = refs[len(widths):]
    acc = res_ref[...]
    k0 = 0
    for a_ref, kw in zip(a_refs, widths):
        acc = acc + jnp.dot(a_ref[...].astype(BF16), w_ref[k0:k0 + kw, :], preferred_element_type=F32)
        k0 += kw
    o_ref[...] = acc


def _matmul_residual(pieces, w, res, name):
    n, dout = res.shape
    widths = tuple(p.shape[1] for p in pieces)
    assert sum(widths) == w.shape[0]
    tm = _tile(n, 512)
    in_specs = [pl.BlockSpec((tm, kw), lambda i: (i, 0)) for kw in widths]
    in_specs += [_resident(w), pl.BlockSpec((tm, dout), lambda i: (i, 0))]
    return pl.pallas_call(
        functools.partial(_mm_res_kernel, widths=widths),
        grid=(n // tm,),
        in_specs=in_specs,
        out_specs=pl.BlockSpec((tm, dout), lambda i: (i, 0)),
        out_shape=jax.ShapeDtypeStruct((n, dout), F32),
        compiler_params=_cparams(("parallel",), 48),
        name=name,
    )(*pieces, w, res)


def _lambda(lq1_ref, lk1_ref, lq2_ref, lk2_ref, lam_init):
    d1 = jnp.sum(lq1_ref[...] * lk1_ref[...], axis=-1, keepdims=True)
    d2 = jnp.sum(lq2_ref[...] * lk2_ref[...], axis=-1, keepdims=True)
    return jnp.exp(d1) - jnp.exp(d2) + lam_init


def _diff_prompt_kernel(qi_ref, ki_ref, q_ref, k_ref, v_ref, lq1_ref, lk1_ref, lq2_ref, lk2_ref, gsub_ref,
                        o_ref, m_ref, l_ref, acc_ref, *, dh, sub, lam_init):
    p = pl.program_id(2)
    qi = qi_ref[p]
    ki = ki_ref[p]
    tq, tk = q_ref.shape[0], k_ref.shape[0]
    dv = 2 * dh

    @pl.when(ki == 0)
    def _init():
        m_ref[...] = jnp.full(m_ref.shape, -jnp.inf, F32)
        l_ref[...] = jnp.zeros(l_ref.shape, F32)
        acc_ref[...] = jnp.zeros(acc_ref.shape, F32)

    def update(r, diagonal):
        rows = slice(r * sub, (r + 1) * sub)
        ncol = (r + 1) * sub if diagonal else tk
        v = v_ref[0:ncol, :]
        if diagonal:
            keep = lax.broadcasted_iota(I32, (sub, sub), 1) <= lax.broadcasted_iota(I32, (sub, sub), 0)
        for mi in range(2):
            s = lax.dot_general(q_ref[rows, mi * dh:(mi + 1) * dh], k_ref[0:ncol, mi * dh:(mi + 1) * dh], _NT,
                                preferred_element_type=F32)
            blocks = [s[:, c * LANE:(c + 1) * LANE] for c in range(ncol // LANE)]
            if diagonal:
                first = ncol // LANE - sub // LANE
                for c in range(sub // LANE):
                    blocks[first + c] = jnp.where(keep[:, c * LANE:(c + 1) * LANE], blocks[first + c], -jnp.inf)
            m_cur = blocks[0]
            for blk in blocks[1:]:
                m_cur = jnp.maximum(m_cur, blk)
            m_prev = m_ref[mi, rows, :]
            m_new = jnp.maximum(m_prev, jnp.max(m_cur, axis=-1, keepdims=True))
            alpha = jnp.exp2(m_prev - m_new)
            l_add = None
            pr = []
            for blk in blocks:
                e = jnp.exp2(blk - m_new)
                l_add = e if l_add is None else l_add + e
                pr.append(e.astype(BF16))
            pv = jnp.dot(jnp.concatenate(pr, axis=1), v, preferred_element_type=F32)
            l_ref[mi, rows, :] = alpha * l_ref[mi, rows, :] + l_add
            acc_ref[mi, rows, :] = jnp.concatenate([alpha] * (dv // LANE), axis=1) * acc_ref[mi, rows, :] + pv
            m_ref[mi, rows, :] = m_new

    @pl.when(ki < qi)
    def _below_diagonal():
        for r in range(tq // sub):
            update(r, False)

    @pl.when(ki == qi)
    def _diagonal():
        lam = _lambda(lq1_ref, lk1_ref, lq2_ref, lk2_ref, lam_init)
        for r in range(tq // sub):
            update(r, True)
            rows = slice(r * sub, (r + 1) * sub)
            l1 = jnp.sum(l_ref[0, rows, :], axis=-1, keepdims=True)
            l2 = jnp.sum(l_ref[1, rows, :], axis=-1, keepdims=True)
            o = acc_ref[0, rows, :] / l1 - lam * (acc_ref[1, rows, :] / l2)
            o_ref[rows, :] = (_rms(o, gsub_ref[...]) * (1.0 - lam_init)).astype(o_ref.dtype)


def _diff_attention_prompt(q, k, v, lams, g_sub, lam_init, batch, seq, out_dtype):
    n, width = q.shape
    dv = width // N_DIFF_HEADS
    dh = dv // 2
    tq = _tile(seq, 1024)
    sub = _tile(tq, 128)
    assert sub % LANE == 0 or tq == sub
    nq = seq // tq
    pairs = [(a, b) for a in range(nq) for b in range(a + 1)]
    qi = jnp.asarray([a for a, _ in pairs], I32)
    ki = jnp.asarray([b for _, b in pairs], I32)
    grid_spec = pltpu.PrefetchScalarGridSpec(
        num_scalar_prefetch=2,
        grid=(batch, N_DIFF_HEADS, len(pairs)),
        in_specs=[
            pl.BlockSpec((tq, dv), lambda b, h, p, qi, ki: (b * nq + qi[p], h)),
            pl.BlockSpec((tq, dv), lambda b, h, p, qi, ki: (b * nq + ki[p], h)),
            pl.BlockSpec((tq, dv), lambda b, h, p, qi, ki: (b * nq + ki[p], h)),
        ] + [pl.BlockSpec((1, dh), lambda b, h, p, qi, ki: (0, 0))] * 4
          + [pl.BlockSpec((1, dv), lambda b, h, p, qi, ki: (0, 0))],
        out_specs=pl.BlockSpec((tq, dv), lambda b, h, p, qi, ki: (b * nq + qi[p], h)),
        scratch_shapes=[pltpu.VMEM((2, tq, LANE), F32), pltpu.VMEM((2, tq, LANE), F32), pltpu.VMEM((2, tq, dv), F32)],
    )
    return pl.pallas_call(
        functools.partial(_diff_prompt_kernel, dh=dh, sub=sub, lam_init=lam_init),
        grid_spec=grid_spec,
        out_shape=jax.ShapeDtypeStruct((n, width), out_dtype),
        compiler_params=_cparams(("parallel", "parallel", "arbitrary"), 40),
        name="diff_attention_prompt",
    )(qi, ki, q, k, v, *lams, g_sub)


def _diff_decode_kernel(pt_ref, q_ref, kn_ref, vn_ref, lq1_ref, lk1_ref, lq2_ref, lk2_ref, gsub_ref, *rest,
                        n_pg, t_new, dh, lam_init):
    k_refs = rest[:n_pg]
    v_refs = rest[n_pg:2 * n_pg]
    o_ref, m_ref, l_ref, acc_ref, qall_ref = rest[2 * n_pg:]
    g = pl.program_id(1)
    n_heads = N_DIFF_HEADS
    dv = 2 * dh
    rows = 2 * t_new
    page = k_refs[0].shape[0]
    kv_rows = page * n_heads
    row_q = lax.broadcasted_iota(I32, (rows, dv), 0)
    lane_q = lax.broadcasted_iota(I32, (rows, dv), 1)

    own_head = ((lax.broadcasted_iota(I32, (n_heads * rows, LANE), 1) & (n_heads - 1))
                == lax.broadcasted_iota(I32, (n_heads * rows, LANE), 0) >> (rows.bit_length() - 1))

    def scores(k2, keep):
        s = lax.dot_general(qall_ref[...], k2, _NT, preferred_element_type=F32)
        return [jnp.where(keep, s[:, c * LANE:(c + 1) * LANE], -jnp.inf) for c in range(k2.shape[0] // LANE)]

    def accumulate(blocks, v_list, first):
        m_cur = blocks[0]
        for blk in blocks[1:]:
            m_cur = jnp.maximum(m_cur, blk)
        m_new = jnp.max(m_cur, axis=-1, keepdims=True)
        if not first:
            m_prev = m_ref[...]
            m_new = jnp.maximum(m_prev, m_new)
        pr = [jnp.exp2(blk - m_new) for blk in blocks]
        l_add = pr[0]
        for e in pr[1:]:
            l_add = l_add + e
        l_add = jnp.sum(l_add, axis=-1, keepdims=True)
        per_v = len(blocks) // len(v_list)
        pv = None
        for jj, vv in enumerate(v_list):
            part = jnp.dot(jnp.concatenate(pr[jj * per_v:(jj + 1) * per_v], axis=1), vv, preferred_element_type=F32)
            pv = part if pv is None else pv + part
        if first:
            l_ref[...] = l_add
            acc_ref[...] = pv
        else:
            alpha = jnp.exp2(m_prev - m_new)
            l_ref[...] = alpha * l_ref[...] + l_add
            acc_ref[...] = alpha * acc_ref[...] + pv
        m_ref[...] = m_new

    @pl.when(g == 0)
    def _new_tokens():
        for h in range(n_heads):
            q = q_ref[0, :, h * dv:(h + 1) * dv].astype(F32)
            q_up = jnp.where((row_q < t_new) & (lane_q < dh), q, 0.0)
            q_dn = jnp.where((row_q >= t_new) & (lane_q >= dh), pltpu.roll(q, t_new, 0), 0.0)
            qall_ref[h * rows:(h + 1) * rows, :] = q_up + q_dn
        pad = jnp.zeros((LANE - rows * n_heads, dv), F32)
        kn = jnp.concatenate([kn_ref[0].reshape(rows * n_heads, dv), pad], axis=0)
        vn = jnp.concatenate([vn_ref[0].reshape(rows * n_heads, dv), pad], axis=0)
        r = lax.broadcasted_iota(I32, (n_heads * rows, LANE), 0) & (rows - 1)
        tok_q = jnp.where(r >= t_new, r - t_new, r)
        tok_k = lax.broadcasted_iota(I32, (n_heads * rows, LANE), 1) >> (n_heads.bit_length() - 1)
        accumulate(scores(kn, own_head & (tok_k <= tok_q)), [vn], True)

    blocks = []
    for kr in k_refs:
        blocks += scores(kr[...].reshape(kv_rows, dv), own_head)
    accumulate(blocks, [vr[...].reshape(kv_rows, dv) for vr in v_refs], False)

    @pl.when(g == pl.num_programs(1) - 1)
    def _finish():
        lam = _lambda(lq1_ref, lk1_ref, lq2_ref, lk2_ref, lam_init)
        a_all = acc_ref[...] / l_ref[...]
        for h in range(n_heads):
            a = a_all[h * rows:(h + 1) * rows]
            o = a - lam * pltpu.roll(a, t_new, 0)
            o = _rms(o, gsub_ref[...]) * (1.0 - lam_init)
            o_ref[0, :, h * dv:(h + 1) * dv] = jnp.where(row_q < t_new, o, 0.0)


def _diff_attention_decode(q, k_new, v_new, cache_k, cache_v, layer, page_table, lams, g_sub, lam_init, t_new):
    batch, rows, width = q.shape
    assert rows == 2 * t_new == SUBLANE, "decode kernel stacks the two softmax maps of the new tokens in one sublane tile"
    dv = width // N_DIFF_HEADS
    dh = dv // 2
    n_pages = page_table.shape[1]
    page = cache_k.shape[2]
    n_pg = next(c for c in (16, 8, 4, 2, 1) if n_pages % c == 0)
    steps = n_pages // n_pg
    seq_blk = lambda b, g, pt: (b, 0, 0)
    seq_blk4 = lambda b, g, pt: (b, 0, 0, 0)
    fixed = lambda b, g, pt: (0, 0)

    def page_spec(jj):
        return pl.BlockSpec((None, None, page, N_DIFF_HEADS, dv),
                            lambda b, g, pt: (layer, pt[b * n_pages + g * n_pg + jj], 0, 0, 0))

    grid_spec = pltpu.PrefetchScalarGridSpec(
        num_scalar_prefetch=1,
        grid=(batch, steps),
        in_specs=[pl.BlockSpec((1, rows, width), seq_blk)]
                 + [pl.BlockSpec((1, rows, N_DIFF_HEADS, dv), seq_blk4)] * 2
                 + [pl.BlockSpec((1, dh), fixed)] * 4 + [pl.BlockSpec((1, dv), fixed)]
                 + [page_spec(jj) for jj in range(n_pg)] * 2,
        out_specs=pl.BlockSpec((1, rows, width), seq_blk),
        scratch_shapes=[pltpu.VMEM((N_DIFF_HEADS * rows, 1), F32), pltpu.VMEM((N_DIFF_HEADS * rows, 1), F32),
                        pltpu.VMEM((N_DIFF_HEADS * rows, dv), F32), pltpu.VMEM((N_DIFF_HEADS * rows, dv), F32)],
    )
    return pl.pallas_call(
        functools.partial(_diff_decode_kernel, n_pg=n_pg, t_new=t_new, dh=dh, lam_init=lam_init),
        grid_spec=grid_spec,
        out_shape=jax.ShapeDtypeStruct((batch, rows, width), F32),
        compiler_params=_cparams(("parallel", "arbitrary"), 48),
        name="diff_attention_decode",
    )(page_table.reshape(-1), q, k_new, v_new, *lams, g_sub, *([cache_k] * n_pg), *([cache_v] * n_pg))


def _gla_chunk(q, k, v, g, s_t, chunk):
    nb = chunk // GLA_SUB
    row1 = lax.broadcasted_iota(I32, (chunk, 1), 0)
    b = g
    step = 1
    while step < chunk:
        b = b + jnp.where(row1 >= step, pltpu.roll(b, step, 0), 0.0)
        step *= 2
    b_last = b[chunk - 1:chunk, :]

    qe = (q * jnp.exp(b)).astype(BF16)
    o = lax.dot_general(qe, s_t.astype(BF16), _NT, preferred_element_type=F32)
    kd = (k * jnp.exp(b_last - b)).astype(BF16)
    s_new = s_t * jnp.exp(b_last) + lax.dot_general(v.astype(BF16), kd, _TN, preferred_element_type=F32)

    row = lax.broadcasted_iota(I32, (chunk, chunk), 0)
    col = lax.broadcasted_iota(I32, (chunk, chunk), 1)
    row_in_blk = row1 & (GLA_SUB - 1)

    att = jnp.where(col == row, jnp.sum(q * k, axis=-1, keepdims=True), 0.0)
    for d in range(1, min(GLA_SUB, chunk)):
        k_sh = pltpu.roll(k, d, 0)
        b_sh = pltpu.roll(b, d, 0)
        decay = jnp.exp(jnp.where(row_in_blk >= d, b - b_sh, -jnp.inf))
        w = jnp.sum(q * k_sh * decay, axis=-1, keepdims=True)
        att = att + jnp.where(col == row - d, w, 0.0)

    dk = q.shape[1]
    for j in range(nb - 1):
        r0 = (j + 1) * GLA_SUB
        b_end = b[r0 - 1:r0, :]
        kj = k[r0 - GLA_SUB:r0] * jnp.exp(b_end - b[r0 - GLA_SUB:r0])
        above = [jnp.zeros((r0 - GLA_SUB, dk), F32)] if j else []
        kj = jnp.concatenate(above + [kj, jnp.zeros((chunk - r0, dk), F32)], axis=0)
        qj = (q[r0:] * jnp.exp(b[r0:] - b_end)).astype(BF16)
        a = lax.dot_general(qj, kj.astype(BF16), _NT, preferred_element_type=F32)
        att = att + jnp.concatenate([jnp.zeros((r0, chunk), F32), a], axis=0)

    o = o + jnp.dot(att.astype(BF16), v.astype(BF16), preferred_element_type=F32)
    return o, s_new


def _gla_kernel(gqk_ref, gv_ref, gr_ref, glog_ref, s0_ref, ggla_ref, o_ref, s_ref, st_ref,
                *, chunk, t_valid, dk, dv, scale):
    tb = pl.program_id(1)
    t_blk = gqk_ref.shape[1]
    n_heads = N_GLA_HEADS

    @pl.when(tb == 0)
    def _load_state():
        for h in range(n_heads):
            st_ref[h] = s0_ref[0, h].T

    g_gla = ggla_ref[...]

    def body(c, carry):
        r0 = pl.multiple_of(c * chunk, chunk)
        rows = pl.ds(r0, chunk)
        pos = tb * t_blk + r0 + lax.broadcasted_iota(I32, (chunk, 1), 0)
        live = pos < t_valid
        for h in range(n_heads):
            q = gqk_ref[0, rows, h * dk:(h + 1) * dk] * scale
            k = gqk_ref[0, rows, (n_heads + h) * dk:(n_heads + h + 1) * dk]
            v = gv_ref[0, rows, h * dv:(h + 1) * dv]
            g = jnp.where(live, glog_ref[0, rows, h * dk:(h + 1) * dk], 0.0)
            k = jnp.where(live, k, 0.0)
            o, s_new = _gla_chunk(q, k, v, g, st_ref[h], chunk)
            st_ref[h] = s_new
            gr = gr_ref[0, rows, h * dv:(h + 1) * dv]
            o_ref[0, rows, h * dv:(h + 1) * dv] = (_rms(o, g_gla) * (gr * _sigmoid(gr))).astype(o_ref.dtype)
        return carry

    lax.fori_loop(0, t_blk // chunk, body, 0)

    @pl.when(tb == pl.num_programs(1) - 1)
    def _store_state():
        for h in range(n_heads):
            s_ref[0, h] = st_ref[h].T


def _gla(gqk, gv, gr, glog, s0, g_gla, t_valid, out_dtype):
    batch, seq, _ = gqk.shape
    dk, dv = s0.shape[2], s0.shape[3]
    chunk = next(c for c in (64, 32, 16, 8) if seq % c == 0)
    t_blk = _tile(seq, 256)
    tok = lambda b, t: (b, t, 0)
    return pl.pallas_call(
        functools.partial(_gla_kernel, chunk=chunk, t_valid=t_valid, dk=dk, dv=dv, scale=dk ** -0.5),
        grid=(batch, seq // t_blk),
        in_specs=[
            pl.BlockSpec((1, t_blk, gqk.shape[2]), tok),
            pl.BlockSpec((1, t_blk, gv.shape[2]), tok),
            pl.BlockSpec((1, t_blk, gr.shape[2]), tok),
            pl.BlockSpec((1, t_blk, glog.shape[2]), tok),
            pl.BlockSpec((1, N_GLA_HEADS, dk, dv), lambda b, t: (b, 0, 0, 0)),
            pl.BlockSpec((1, dv), lambda b, t: (0, 0)),
        ],
        out_specs=[
            pl.BlockSpec((1, t_blk, gv.shape[2]), tok),
            pl.BlockSpec((1, N_GLA_HEADS, dk, dv), lambda b, t: (b, 0, 0, 0)),
        ],
        out_shape=[jax.ShapeDtypeStruct(gv.shape, out_dtype), jax.ShapeDtypeStruct(s0.shape, F32)],
        scratch_shapes=[pltpu.VMEM((N_GLA_HEADS, dv, dk), F32)],
        compiler_params=_cparams(("parallel", "arbitrary"), 32),
        name="gla",
    )(gqk, gv, gr, glog, s0, g_gla)


def _cross_kernel(q_ref, mk_ref, mv_ref, o_ref, kh_ref, vh_ref, *, dh):
    @pl.when(pl.program_id(1) == 0)
    def _split_heads():
        for h in range(N_MEM_HEADS):
            kh_ref[h] = mk_ref[0, :, h, :].astype(BF16)
            vh_ref[h] = mv_ref[0, :, h, :].astype(BF16)

    for h in range(N_MEM_HEADS):
        sl = slice(h * dh, (h + 1) * dh)
        s = lax.dot_general(q_ref[0, :, sl].astype(BF16), kh_ref[h], _NT, preferred_element_type=F32)
        pr = jnp.exp(s - jnp.max(s, axis=-1, keepdims=True))
        pr = pr / jnp.sum(pr, axis=-1, keepdims=True)
        o_ref[0, :, sl] = jnp.dot(pr.astype(BF16), vh_ref[h], preferred_element_type=F32).astype(o_ref.dtype)


def _cross_few_rows_kernel(q_ref, mk_ref, mv_ref, o_ref, *, dh):
    rows = q_ref.shape[1]
    mem = mk_ref.shape[1]
    q = jnp.concatenate([q_ref[0, :, h * dh:(h + 1) * dh].astype(F32) for h in range(N_MEM_HEADS)], axis=0)
    k2 = mk_ref[0].reshape(mem * N_MEM_HEADS, dh)
    v2 = mv_ref[0].reshape(mem * N_MEM_HEADS, dh)
    s = lax.dot_general(q, k2, _NT, preferred_element_type=F32)
    own_head = ((lax.broadcasted_iota(I32, s.shape, 1) & (N_MEM_HEADS - 1))
                == lax.broadcasted_iota(I32, s.shape, 0) >> (rows.bit_length() - 1))
    s = jnp.where(own_head, s, -jnp.inf)
    pr = jnp.exp(s - jnp.max(s, axis=-1, keepdims=True))
    pr = pr / jnp.sum(pr, axis=-1, keepdims=True)
    o = jnp.dot(pr, v2, preferred_element_type=F32)
    for h in range(N_MEM_HEADS):
        o_ref[0, :, h * dh:(h + 1) * dh] = o[h * rows:(h + 1) * rows].astype(o_ref.dtype)


def _cross_attention(q, mk, mv, mem_index):
    batch, seq, d = q.shape
    mem, n_heads, dh = mk.shape[-3:]
    tq = _tile(seq, 512)
    lead = mk.ndim - 4
    mem_spec = pl.BlockSpec((None,) * lead + (1, mem, n_heads, dh), lambda b, t: mem_index(b) + (0, 0, 0))
    if seq == SUBLANE:
        return pl.pallas_call(
            functools.partial(_cross_few_rows_kernel, dh=dh),
            grid=(batch, 1),
            in_specs=[pl.BlockSpec((1, seq, d), lambda b, t: (b, 0, 0)), mem_spec, mem_spec],
            out_specs=pl.BlockSpec((1, seq, d), lambda b, t: (b, 0, 0)),
            out_shape=jax.ShapeDtypeStruct(q.shape, F32),
            compiler_params=_cparams(("parallel", "arbitrary"), 40),
            name="cross_attention_few_rows",
        )(q, mk, mv)
    return pl.pallas_call(
        functools.partial(_cross_kernel, dh=dh),
        grid=(batch, seq // tq),
        in_specs=[pl.BlockSpec((1, tq, d), lambda b, t: (b, t, 0)), mem_spec, mem_spec],
        out_specs=pl.BlockSpec((1, tq, d), lambda b, t: (b, t, 0)),
        out_shape=jax.ShapeDtypeStruct(q.shape, BF16 if tq % 16 == 0 else F32),
        scratch_shapes=[pltpu.VMEM((n_heads, mem, dh), BF16), pltpu.VMEM((n_heads, mem, dh), BF16)],
        compiler_params=_cparams(("parallel", "arbitrary"), 40),
        name="cross_attention",
    )(q, mk, mv)


def _two_group_specs(n_a, n_b, tm, d):
    ta = n_a // tm
    return (pl.BlockSpec((tm, d), lambda i, *_: (jnp.minimum(i, ta - 1), 0)),
            pl.BlockSpec((tm, d), lambda i, *_: (jnp.maximum(i - ta, 0), 0)))


def _route_kernel(xa_ref, xb_ref, g_ref, wh_ref, wl_ref, xn_ref, idx_ref, wgt_ref, *, tiles_a):
    x = jnp.where(pl.program_id(0) < tiles_a, xa_ref[...], xb_ref[...])
    xn = _rms(x, g_ref[...])
    xn_ref[...] = xn
    xh = xn.astype(BF16)
    xl = (xn - xh.astype(F32)).astype(BF16)
    logits = (jnp.dot(xh, wh_ref[...], preferred_element_type=F32)
              + jnp.dot(xl, wh_ref[...], preferred_element_type=F32)
              + jnp.dot(xh, wl_ref[...], preferred_element_type=F32))
    lane = lax.broadcasted_iota(I32, logits.shape, 1)
    lane_f = lane.astype(F32)
    neg = -jnp.inf

    def first_max(vals):
        top = jnp.max(vals, axis=-1, keepdims=True)
        first = jnp.min(jnp.where(vals == top, lane_f, float(LANE)), axis=-1, keepdims=True)
        return top, first.astype(I32)

    is_grp = lane < N_GROUPS
    g_top, g_idx = first_max(jnp.where(is_grp, logits, neg))
    g_w = 1.0 / jnp.sum(jnp.where(is_grp, jnp.exp(logits - g_top), 0.0), axis=-1, keepdims=True)
    lo = N_GROUPS + g_idx * EXPERTS_PER_GROUP
    le = jnp.where((lane >= lo) & (lane < lo + EXPERTS_PER_GROUP), logits, neg)
    v1, i1 = first_max(le)
    v2, i2 = first_max(jnp.where(lane == i1, neg, le))
    e2 = jnp.exp(v2 - v1)
    w1 = g_w / (1.0 + e2)
    w2 = g_w * e2 / (1.0 + e2)
    idx_ref[...] = jnp.where(lane == 0, i1 - N_GROUPS, jnp.where(lane == 1, i2 - N_GROUPS, 0))
    wgt_ref[...] = jnp.where(lane == 0, w1, jnp.where(lane == 1, w2, 0.0))


def _route(xa, xb, tm, g_ffn, w_hi, w_lo):
    d = xa.shape[1]
    n = xa.shape[0] + xb.shape[0]
    row = lambda i: (i, 0)
    fixed = lambda i: (0, 0)
    return pl.pallas_call(
        functools.partial(_route_kernel, tiles_a=xa.shape[0] // tm),
        grid=(n // tm,),
        in_specs=[*_two_group_specs(xa.shape[0], xb.shape[0], tm, d), pl.BlockSpec((1, d), fixed),
                  pl.BlockSpec((d, LANE), fixed), pl.BlockSpec((d, LANE), fixed)],
        out_specs=[pl.BlockSpec((tm, d), row), pl.BlockSpec((tm, LANE), row), pl.BlockSpec((tm, LANE), row)],
        out_shape=[jax.ShapeDtypeStruct((n, d), F32), jax.ShapeDtypeStruct((n, LANE), I32),
                   jax.ShapeDtypeStruct((n, LANE), F32)],
        compiler_params=_cparams(("parallel",), 40),
        name="moe_route",
    )(xa, xb, g_ffn, w_hi, w_lo)


def _start_row_copies(n_rows, make_copy):
    def start(it, c):
        for u in range(DMA_UNROLL):
            for kk in range(2):
                make_copy(it * DMA_UNROLL + u, kk).start(priority=kk)
        return c

    assert n_rows % DMA_UNROLL == 0
    lax.fori_loop(0, n_rows // DMA_UNROLL, start, 0)


def _wait_row_copies(n_rows, make_copy):
    def wait(it, c):
        for u in range(DMA_UNROLL):
            for kk in range(2):
                make_copy(it * DMA_UNROLL + u, kk).wait()
        return c

    lax.fori_loop(0, n_rows // DMA_UNROLL, wait, 0)


def _dispatch_kernel(pos_ref, pend_ref, x_ref, xs_ref, zero_ref, sem, zsem, *, tile_e, tm):
    i = pl.program_id(0)

    @pl.when(i == 0)
    def _zero_padding():
        zero_ref[...] = jnp.zeros(zero_ref.shape, F32)

        def fill(e):
            start = pl.multiple_of(jnp.maximum(pend_ref[e] - tile_e, 0), tile_e)
            return pltpu.make_async_copy(zero_ref, xs_ref.at[pl.ds(start, tile_e), :], zsem)

        for e in range(N_EXPERTS):
            fill(e).start()
        for e in range(N_EXPERTS):
            fill(e).wait()

        def tail(t):
            return pltpu.make_async_copy(zero_ref, xs_ref.at[pl.ds(pl.multiple_of(t * tile_e, tile_e), tile_e), :], zsem)

        first_unused = pend_ref[N_EXPERTS - 1] // tile_e
        n_tiles = xs_ref.shape[0] // tile_e
        lax.fori_loop(first_unused, n_tiles, lambda t, c: (tail(t).start(), c)[1], 0)
        lax.fori_loop(first_unused, n_tiles, lambda t, c: (tail(t).wait(), c)[1], 0)

    def scatter(step):
        slot = step & 1

        def row_copy(r, kk):
            tok = step * tm + r
            dst = pos_ref[tok * 2 + kk]
            return pltpu.make_async_copy(x_ref.at[pl.ds(tok, 1), :], xs_ref.at[pl.ds(dst, 1), :], sem.at[slot, kk])
        return row_copy

    _start_row_copies(tm, scatter(i))

    @pl.when(i > 0)
    def _():
        _wait_row_copies(tm, scatter(i - 1))

    @pl.when(i == pl.num_programs(0) - 1)
    def _():
        _wait_row_copies(tm, scatter(i))


def _dispatch(xn, tm, pos, p_end, n_slots, tile_e):
    n, d = xn.shape
    grid_spec = pltpu.PrefetchScalarGridSpec(
        num_scalar_prefetch=2,
        grid=(n // tm,),
        in_specs=[pl.BlockSpec(memory_space=pl.ANY)],
        out_specs=pl.BlockSpec(memory_space=pl.ANY),
        scratch_shapes=[pltpu.VMEM((tile_e, d), F32), pltpu.SemaphoreType.DMA((2, 2)), pltpu.SemaphoreType.DMA(())],
    )
    return pl.pallas_call(
        functools.partial(_dispatch_kernel, tile_e=tile_e, tm=tm),
        grid_spec=grid_spec,
        out_shape=jax.ShapeDtypeStruct((n_slots, d), F32),
        compiler_params=_cparams(("arbitrary",), 32),
        name="moe_dispatch",
    )(pos, p_end, xn)


def _experts_kernel(te_ref, nu_ref, first_ref, next_ref, slot_ref, x_ref, wg_hbm, wu_hbm, wd_hbm, y_ref,
                    wgf_ref, wuf_ref, wdf_ref, wgb_ref, wub_ref, wdb_ref, sem):
    t = pl.program_id(0)

    def fetch(e, slot):
        return [pltpu.make_async_copy(src.at[0, e], dst.at[slot], sem.at[slot, j])
                for j, (src, dst) in enumerate(((wg_hbm, wgf_ref), (wu_hbm, wuf_ref), (wd_hbm, wdf_ref)))]

    @pl.when(t == 0)
    def _first_fetch():
        for c in fetch(te_ref[0], slot_ref[0]):
            c.start()

    @pl.when(first_ref[t] == 1)
    def _new_expert():
        slot = slot_ref[t]
        for c in fetch(te_ref[t], slot):
            c.wait()

        @pl.when(next_ref[t] >= 0)
        def _prefetch():
            for c in fetch(next_ref[t], 1 - slot):
                c.start()

        wgb_ref[...] = wgf_ref[slot].astype(BF16)
        wub_ref[...] = wuf_ref[slot].astype(BF16)
        wdb_ref[...] = wdf_ref[slot].astype(BF16)

    @pl.when(t < nu_ref[0])
    def _mlp():
        x = x_ref[...].astype(BF16)
        hg = jnp.dot(x, wgb_ref[...], preferred_element_type=F32)
        hu = jnp.dot(x, wub_ref[...], preferred_element_type=F32)
        h = (hg * _sigmoid(hg)) * hu
        y_ref[...] = jnp.dot(h.astype(BF16), wdb_ref[...], preferred_element_type=F32)

    @pl.when(t >= nu_ref[0])
    def _unused():
        y_ref[...] = jnp.zeros(y_ref.shape, F32)


def _experts(xs, tile_expert, n_used, w_gate, w_up, w_down, tm):
    n_slots, d = xs.shape
    de = w_gate.shape[-1]
    assert w_gate.ndim == 4 and w_gate.shape[0] == 1, "weights come as (1, experts, rows, cols)"
    n_tiles = n_slots // tm
    idx = jnp.arange(n_tiles, dtype=I32)
    first = jnp.concatenate([jnp.ones((1,), I32), (tile_expert[1:] != tile_expert[:-1]).astype(I32)])
    slot = (jnp.cumsum(first) - 1) & 1
    run_start = jnp.where(first == 1, idx, n_tiles)
    next_start = jnp.concatenate([lax.cummin(run_start, reverse=True)[1:], jnp.full((1,), n_tiles, I32)])
    next_expert = jnp.where(next_start < n_tiles, tile_expert[jnp.minimum(next_start, n_tiles - 1)], -1)
    any_spec = pl.BlockSpec(memory_space=pl.ANY)
    grid_spec = pltpu.PrefetchScalarGridSpec(
        num_scalar_prefetch=5,
        grid=(n_tiles,),
        in_specs=[pl.BlockSpec((tm, d), lambda t, te, nu, *_: (jnp.minimum(t, nu[0] - 1), 0)),
                  any_spec, any_spec, any_spec],
        out_specs=pl.BlockSpec((tm, d), lambda t, *_: (t, 0)),
        scratch_shapes=[pltpu.VMEM((2, d, de), F32), pltpu.VMEM((2, d, de), F32), pltpu.VMEM((2, de, d), F32),
                        pltpu.VMEM((d, de), BF16), pltpu.VMEM((d, de), BF16), pltpu.VMEM((de, d), BF16),
                        pltpu.SemaphoreType.DMA((2, 3))],
    )
    return pl.pallas_call(
        _experts_kernel,
        grid_spec=grid_spec,
        out_shape=jax.ShapeDtypeStruct((n_slots, d), F32),
        compiler_params=_cparams(("arbitrary",), 52),
        name="moe_experts",
    )(tile_expert, n_used, first, next_expert.astype(I32), slot.astype(I32), xs, w_gate, w_up, w_down)


def _combine_kernel(pos_ref, xa_ref, xb_ref, wgt_ref, g_ref, ys_ref, oa_ref, ob_ref, buf_ref, sem, *, tiles_a):
    i = pl.program_id(0)
    tm = xa_ref.shape[0]

    def gather(step):
        slot = step & 1

        def row_copy(r, kk):
            src = pos_ref[(step * tm + r) * 2 + kk]
            return pltpu.make_async_copy(ys_ref.at[pl.ds(src, 1), :], buf_ref.at[slot, kk, pl.ds(r, 1), :],
                                         sem.at[slot, kk])
        return row_copy

    @pl.when(i == 0)
    def _():
        _start_row_copies(tm, gather(i))

    @pl.when(i + 1 < pl.num_programs(0))
    def _():
        _start_row_copies(tm, gather(i + 1))

    _wait_row_copies(tm, gather(i))
    slot = i & 1
    wgt = wgt_ref[...]
    first_group = i < tiles_a
    x = jnp.where(first_group, xa_ref[...], xb_ref[...])
    y = _rms(x + (wgt[:, 0:1] * buf_ref[slot, 0] + wgt[:, 1:2] * buf_ref[slot, 1]), g_ref[...])

    @pl.when(first_group)
    def _():
        oa_ref[...] = y

    @pl.when(jnp.logical_not(first_group))
    def _():
        ob_ref[...] = y


def _combine(xa, xb, tm, wgt, g_final, ys, pos):
    d = xa.shape[1]
    n = xa.shape[0] + xb.shape[0]
    grid_spec = pltpu.PrefetchScalarGridSpec(
        num_scalar_prefetch=1,
        grid=(n // tm,),
        in_specs=[*_two_group_specs(xa.shape[0], xb.shape[0], tm, d), pl.BlockSpec((tm, LANE), lambda i, pos: (i, 0)),
                  pl.BlockSpec((1, d), lambda i, pos: (0, 0)), pl.BlockSpec(memory_space=pl.ANY)],
        out_specs=list(_two_group_specs(xa.shape[0], xb.shape[0], tm, d)),
        scratch_shapes=[pltpu.VMEM((2, 2, tm, d), F32), pltpu.SemaphoreType.DMA((2, 2))],
    )
    return pl.pallas_call(
        functools.partial(_combine_kernel, tiles_a=xa.shape[0] // tm),
        grid_spec=grid_spec,
        out_shape=[jax.ShapeDtypeStruct(xa.shape, F32), jax.ShapeDtypeStruct(xb.shape, F32)],
        compiler_params=_cparams(("arbitrary",), 40),
        name="moe_combine",
    )(pos, xa, xb, wgt, g_final, ys)


def _moe_and_final_norm(xa, xb, g_ffn, w_route_hi, w_route_lo, w_gate, w_up, w_down, g_final):
    d = xa.shape[1]
    n = xa.shape[0] + xb.shape[0]
    tm_tok = _tile(math.gcd(xa.shape[0], xb.shape[0]), 256)
    assert tm_tok % DMA_UNROLL == 0
    xn, ridx, rw = _route(xa, xb, tm_tok, g_ffn, w_route_hi, w_route_lo)
    tm = 256 if n >= 2048 else 128
    e = ridx[:, :2].reshape(-1)
    onehot = (e[:, None] == jnp.arange(N_EXPERTS, dtype=I32)[None, :]).astype(I32)
    csum = jnp.cumsum(onehot, axis=0)
    counts = csum[-1]
    padded = (counts + tm - 1) // tm * tm
    p_end = jnp.cumsum(padded).astype(I32)
    pos = jnp.sum(onehot * ((p_end - padded)[None, :] + csum - 1), axis=1).astype(I32)
    n_tiles = (2 * n + N_EXPERTS * (tm - 1)) // tm + 1
    tile_start = jnp.arange(n_tiles, dtype=I32) * tm
    tile_expert = jnp.minimum(jnp.sum((tile_start[:, None] >= p_end[None, :]).astype(I32), axis=1), N_EXPERTS - 1)
    n_used = (p_end[-1:] // tm).astype(I32)
    xs = _dispatch(xn, tm_tok, pos, p_end, n_tiles * tm, tm)
    ys = _experts(xs, tile_expert.astype(I32), n_used, w_gate, w_up, w_down, tm)
    return _combine(xa, xb, tm_tok, rw, g_final, ys, pos)


def _layer(x3, s0, t_valid, mk, mv, mem_index, diff_fn, lw):
    batch, seq, d = x3.shape
    n = batch * seq
    x = x3.reshape(n, d)
    qs, k4, v4, kb, vb, gqk, gv, gr, glog = _in_projection(x, lw["g_mix"], lw["w_in_main"], lw["w_in_ga"],
                                                            lw["w_a2p"], lw["b_a"])
    d_out = diff_fn(qs, k4, v4, kb, vb)
    r3 = lambda a: a.reshape(batch, seq, a.shape[-1])
    g_out, s_new = _gla(r3(gqk), r3(gv), r3(gr), r3(glog), s0, lw["g_gla"], t_valid,
                        BF16 if seq % 16 == 0 else F32)
    x1 = _matmul_residual([d_out.reshape(n, -1), g_out.reshape(n, -1)], lw["w_o"], x, "mixer_out")
    dh_mem = d // N_MEM_HEADS
    xq = _norm_matmul(x1, lw["g_cross"], lw["w_q_mem"], BF16 if seq % 16 == 0 else F32, "cross_q",
                      out_scale=dh_mem ** -0.5)
    ca = _cross_attention(xq.reshape(batch, seq, d), mk, mv, mem_index)
    x2 = _matmul_residual([ca.reshape(n, d)], lw["w_o_mem"], x1, "cross_out")
    return x2, k4, v4, s_new


def kernel(x_prompt, x_sample, mem_prompt, cache_k, cache_v, state_gla, cache_mem_k, cache_mem_v, page_table, g_mix, w_in, w_a2, b_a, lambda_q1, lambda_k1, lambda_q2, lambda_k2, g_subln, g_gla, w_o, g_mem, g_cross, w_q_mem, w_k_mem, w_v_mem, w_o_mem, g_ffn, w_group, w_router, w_gate, w_up, w_down, g_final):
    depth = w_in.shape[0]
    batch, seq, d = x_prompt.shape
    dec_batch, dec_seq, _ = x_sample.shape
    mem_len = mem_prompt.shape[1]
    width = d // 2
    main_cols = w_in.shape[2] - GATE_RANK
    assert main_cols == 6 * width and w_a2.shape[1:] == (GATE_RANK, width // 2)
    dk_gla = width // N_GLA_HEADS // 2
    dv_gla = width // N_GLA_HEADS
    dv_diff = width // N_DIFF_HEADS
    dec_pad = -(-dec_seq // SUBLANE) * SUBLANE

    xp = x_prompt
    xs = jnp.pad(x_sample, ((0, 0), (0, dec_pad - dec_seq), (0, 0)))
    s0_prompt = jnp.zeros((batch, N_GLA_HEADS, dk_gla, dv_gla), F32)
    row2 = lambda a: a.reshape(1, -1)
    outs = {name: [] for name in ("kp", "vp", "sp", "mkp", "mvp", "ks", "vs", "ss")}
    for l in range(depth):
        lam_init = 0.8 - 0.6 * math.exp(-0.3 * l)
        lw = dict(
            g_mix=row2(g_mix[l]), g_gla=row2(g_gla[l]), g_cross=row2(g_cross[l]), b_a=row2(b_a[l]),
            w_in_main=w_in[l][:, :main_cols].astype(BF16),
            w_in_ga=jnp.pad(w_in[l][:, main_cols:], ((0, 0), (0, LANE - GATE_RANK))).astype(BF16),
            w_a2p=jnp.pad(w_a2[l], ((0, LANE - GATE_RANK), (0, 0))).astype(BF16),
            w_o=w_o[l].astype(BF16), w_q_mem=w_q_mem[l].astype(BF16), w_o_mem=w_o_mem[l].astype(BF16),
        )
        lams = (row2(lambda_q1[l]), row2(lambda_k1[l]), row2(lambda_q2[l]), row2(lambda_k2[l]))
        g_sub = row2(g_subln[l])
        w_route = jnp.pad(jnp.concatenate([w_group[l], w_router[l]], axis=1),
                          ((0, 0), (0, LANE - N_GROUPS - N_EXPERTS)))
        w_route_hi = w_route.astype(BF16)
        w_route_lo = (w_route - w_route_hi.astype(F32)).astype(BF16)

        mem2 = mem_prompt.reshape(batch * mem_len, d)
        mkp = _norm_matmul(mem2, row2(g_mem[l]), w_k_mem[l].astype(BF16), F32, "mem_k", n_heads=N_MEM_HEADS)
        mvp = _norm_matmul(mem2, row2(g_mem[l]), w_v_mem[l].astype(BF16), F32, "mem_v", n_heads=N_MEM_HEADS)
        mkp = mkp.reshape(batch, mem_len, N_MEM_HEADS, -1)
        mvp = mvp.reshape(batch, mem_len, N_MEM_HEADS, -1)

        prompt_attn = lambda qs, k4, v4, kb, vb: _diff_attention_prompt(qs, kb, vb, lams, g_sub, lam_init, batch, seq,
                                                                        BF16)
        xp2, kp, vp, sp = _layer(xp, s0_prompt, seq, mkp, mvp, lambda b: (b,), prompt_attn, lw)

        sample_attn = lambda qs, k4, v4, kb, vb, l=l: _diff_attention_decode(
            qs.reshape(dec_batch, dec_pad, width), k4.reshape(dec_batch, dec_pad, N_DIFF_HEADS, dv_diff),
            v4.reshape(dec_batch, dec_pad, N_DIFF_HEADS, dv_diff), cache_k, cache_v, l, page_table, lams, g_sub,
            lam_init, dec_seq)
        xs2, ks, vs, ss = _layer(xs, state_gla[l], dec_seq, cache_mem_k, cache_mem_v, lambda b, l=l: (l, b),
                                 sample_attn, lw)

        assert l == depth - 1, "the MoE combine kernel fuses the final norm; deeper stacks need an un-normed variant"
        moe = functools.partial(_moe_and_final_norm, g_ffn=row2(g_ffn[l]), w_route_hi=w_route_hi,
                                w_route_lo=w_route_lo, w_gate=w_gate[l:l + 1], w_up=w_up[l:l + 1],
                                w_down=w_down[l:l + 1], g_final=row2(g_final))
        xp, xs = moe(xp2, xs2)
        xp = xp.reshape(batch, seq, d)
        xs = xs.reshape(dec_batch, dec_pad, d)

        outs["kp"].append(kp.reshape(batch, seq, N_DIFF_HEADS, dv_diff))
        outs["vp"].append(vp.reshape(batch, seq, N_DIFF_HEADS, dv_diff))
        outs["sp"].append(sp)
        outs["mkp"].append(mkp)
        outs["mvp"].append(mvp)
        outs["ks"].append(ks.reshape(dec_batch, dec_pad, N_DIFF_HEADS, dv_diff)[:, :dec_seq])
        outs["vs"].append(vs.reshape(dec_batch, dec_pad, N_DIFF_HEADS, dv_diff)[:, :dec_seq])
        outs["ss"].append(ss)

    st = lambda name: jnp.stack(outs[name])
    return (xp, xs[:, :dec_seq], st("kp"), st("vp"), st("sp"), st("mkp"), st("mvp"), st("ks"), st("vs"), st("ss"))
```

```python
import functools
import math

import jax
import jax.numpy as jnp
from jax import lax
from jax.experimental import pallas as pl
from jax.experimental.pallas import tpu as pltpu

F32 = jnp.float32
BF16 = jnp.bfloat16
I32 = jnp.int32

N_DIFF_HEADS = 4
N_GLA_HEADS = 4
GATE_RANK = 16
GATE_TAU = 16.0
N_MEM_HEADS = 4
N_GROUPS = 4
EXPERTS_PER_GROUP = 8
N_EXPERTS = N_GROUPS * EXPERTS_PER_GROUP
RMS_EPS = 1e-6
GLA_SUB = 8
LOG2E = math.log2(math.e)

LANE = 128
SUBLANE = 8
MIB = 1024 * 1024
DMA_UNROLL = 8

_NT = (((1,), (1,)), ((), ()))
_TN = (((0,), (0,)), ((), ()))


def _cparams(semantics, vmem_mib):
    return pltpu.CompilerParams(dimension_semantics=semantics, vmem_limit_bytes=int(vmem_mib * MIB))


def _tile(n, target):
    if n <= target:
        return n
    t = target - target % SUBLANE
    while t >= SUBLANE:
        if n % t == 0:
            return t
        t -= SUBLANE
    raise ValueError(f"no sublane-aligned tile of {n} below {target}")


def _rms(x, g):
    ms = jnp.mean(x * x, axis=-1, keepdims=True)
    return x * lax.rsqrt(ms + RMS_EPS) * g


def _sigmoid(x):
    return 1.0 / (1.0 + jnp.exp(-x))


def _resident(a):
    return pl.BlockSpec(a.shape, lambda *_: (0,) * a.ndim, pipeline_mode=pl.Buffered(1))


def _store_heads(o_ref, val, n_heads):
    w = val.shape[1] // n_heads
    for h in range(n_heads):
        o_ref[:, h, :] = val[:, h * w:(h + 1) * w].astype(o_ref.dtype)


def _inproj_kernel(x_ref, g_ref, w_ref, wga_ref, wa2_ref, ba_ref,
                   qs_ref, k_ref, v_ref, kb_ref, vb_ref, gqk_ref, gv_ref, gr_ref, glog_ref, *, q_scale):
    xn = _rms(x_ref[...], g_ref[...]).astype(BF16)
    wblk = qs_ref.shape[1]

    def proj(group):
        return jnp.dot(xn, w_ref[:, group * wblk:(group + 1) * wblk], preferred_element_type=F32)

    qs_ref[...] = (proj(0) * q_scale).astype(BF16)
    for group, (o_ref, ob_ref) in ((1, (k_ref, kb_ref)), (2, (v_ref, vb_ref))):
        r = proj(group)
        _store_heads(o_ref, r, N_DIFF_HEADS)
        ob_ref[...] = r.astype(BF16)
    for group, o_ref in ((3, gqk_ref), (4, gv_ref), (5, gr_ref)):
        o_ref[...] = proj(group)
    ga = jnp.dot(xn, wga_ref[...], preferred_element_type=F32)
    z = jnp.dot(ga.astype(BF16), wa2_ref[...], preferred_element_type=F32) + ba_ref[...]
    glog_ref[...] = (jnp.minimum(z, 0.0) - jnp.log1p(jnp.exp(-jnp.abs(z)))) * (1.0 / GATE_TAU)


def _in_projection(x, g_mix, w_main, w_ga, w_a2p, b_a):
    n, d = x.shape
    wblk = d // 2
    assert w_main.shape[1] == 6 * wblk
    dv = wblk // N_DIFF_HEADS
    gk = w_a2p.shape[1]
    tm = _tile(n, 256)
    row = lambda i: (i, 0)
    row3 = lambda i: (i, 0, 0)
    flat = lambda dt: (jax.ShapeDtypeStruct((n, wblk), dt), pl.BlockSpec((tm, wblk), row))
    heads = (jax.ShapeDtypeStruct((n, N_DIFF_HEADS, dv), F32), pl.BlockSpec((tm, N_DIFF_HEADS, dv), row3))
    outs = [flat(BF16), heads, heads, flat(BF16), flat(BF16), flat(F32), flat(F32), flat(F32),
            (jax.ShapeDtypeStruct((n, gk), F32), pl.BlockSpec((tm, gk), row))]
    return pl.pallas_call(
        functools.partial(_inproj_kernel, q_scale=(dv // 2) ** -0.5 * LOG2E),
        grid=(n // tm,),
        in_specs=[pl.BlockSpec((tm, d), row), _resident(g_mix), _resident(w_main), _resident(w_ga),
                  _resident(w_a2p), _resident(b_a)],
        out_specs=[o[1] for o in outs],
        out_shape=[o[0] for o in outs],
        compiler_params=_cparams(("parallel",), 56),
        name="in_projection",
    )(x, g_mix, w_main, w_ga, w_a2p, b_a)


def _norm_mm_kernel(x_ref, g_ref, w_ref, o_ref, *, out_scale, n_heads):
    xn = _rms(x_ref[...], g_ref[...]).astype(BF16)
    r = jnp.dot(xn, w_ref[...], preferred_element_type=F32)
    if out_scale != 1.0:
        r = r * out_scale
    if n_heads:
        _store_heads(o_ref, r, n_heads)
    else:
        o_ref[...] = r.astype(o_ref.dtype)


def _norm_matmul(x, g, w, out_dtype, name, out_scale=1.0, n_heads=0):
    n, d = x.shape
    dout = w.shape[1]
    tm = _tile(n, 512)
    if n_heads:
        out_spec = pl.BlockSpec((tm, n_heads, dout // n_heads), lambda i: (i, 0, 0))
        out_shape = jax.ShapeDtypeStruct((n, n_heads, dout // n_heads), out_dtype)
    else:
        out_spec = pl.BlockSpec((tm, dout), lambda i: (i, 0))
        out_shape = jax.ShapeDtypeStruct((n, dout), out_dtype)
    return pl.pallas_call(
        functools.partial(_norm_mm_kernel, out_scale=out_scale, n_heads=n_heads),
        grid=(n // tm,),
        in_specs=[pl.BlockSpec((tm, d), lambda i: (i, 0)), _resident(g), _resident(w)],
        out_specs=out_spec,
        out_shape=out_shape,
        compiler_params=_cparams(("parallel",), 48),
        name=name,
    )(x, g, w)


def _mm_res_kernel(*refs, widths):
    a_refs = refs[:len(widths)]
    w_ref, res_ref, o_ref = refs[len(widths):]
    acc = res_ref[...]
    k0 = 0
    for a_ref, kw in zip(a_refs, widths):
        acc = acc + jnp.dot(a_ref[...].astype(BF16), w_ref[k0:k0 + kw, :], preferred_element_type=F32)
        k0 += kw
    o_ref[...] = acc


def _matmul_residual(pieces, w, res, name):
    n, dout = res.shape
    widths = tuple(p.shape[1] for p in pieces)
    assert sum(widths) == w.shape[0]
    tm = _tile(n, 512)
    in_specs = [pl.BlockSpec((tm, kw), lambda i: (i, 0)) for kw in widths]
    in_specs += [_resident(w), pl.BlockSpec((tm, dout), lambda i: (i, 0))]
    return pl.pallas_call(
        functools.partial(_mm_res_kernel, widths=widths),
        grid=(n // tm,),
        in_specs=in_specs,
        out_specs=pl.BlockSpec((tm, dout), lambda i: (i, 0)),
        out_shape=jax.ShapeDtypeStruct((n, dout), F32),
        compiler_params=_cparams(("parallel",), 48),
        name=name,
    )(*pieces, w, res)


def _lambda(lq1_ref, lk1_ref, lq2_ref, lk2_ref, lam_init):
    d1 = jnp.sum(lq1_ref[...] * lk1_ref[...], axis=-1, keepdims=True)
    d2 = jnp.sum(lq2_ref[...] * lk2_ref[...], axis=-1, keepdims=True)
    return jnp.exp(d1) - jnp.exp(d2) + lam_init


def _diff_prompt_kernel(qi_ref, ki_ref, q_ref, k_ref, v_ref, lq1_ref, lk1_ref, lq2_ref, lk2_ref, gsub_ref,
                        o_ref, m_ref, l_ref, acc_ref, *, dh, sub, lam_init):
    p = pl.program_id(2)
    qi = qi_ref[p]
    ki = ki_ref[p]
    tq, tk = q_ref.shape[0], k_ref.shape[0]
    dv = 2 * dh

    @pl.when(ki == 0)
    def _init():
        m_ref[...] = jnp.full(m_ref.shape, -jnp.inf, F32)
        l_ref[...] = jnp.zeros(l_ref.shape, F32)
        acc_ref[...] = jnp.zeros(acc_ref.shape, F32)

    def update(r, diagonal):
        rows = slice(r * sub, (r + 1) * sub)
        ncol = (r + 1) * sub if diagonal else tk
        v = v_ref[0:ncol, :]
        if diagonal:
            keep = lax.broadcasted_iota(I32, (sub, sub), 1) <= lax.broadcasted_iota(I32, (sub, sub), 0)
        for mi in range(2):
            s = lax.dot_general(q_ref[rows, mi * dh:(mi + 1) * dh], k_ref[0:ncol, mi * dh:(mi + 1) * dh], _NT,
                                preferred_element_type=F32)
            blocks = [s[:, c * LANE:(c + 1) * LANE] for c in range(ncol // LANE)]
            if diagonal:
                first = ncol // LANE - sub // LANE
                for c in range(sub // LANE):
                    blocks[first + c] = jnp.where(keep[:, c * LANE:(c + 1) * LANE], blocks[first + c], -jnp.inf)
            m_cur = blocks[0]
            for blk in blocks[1:]:
                m_cur = jnp.maximum(m_cur, blk)
            m_prev = m_ref[mi, rows, :]
            m_new = jnp.maximum(m_prev, jnp.max(m_cur, axis=-1, keepdims=True))
            alpha = jnp.exp2(m_prev - m_new)
            l_add = None
            pr = []
            for blk in blocks:
                e = jnp.exp2(blk - m_new)
                l_add = e if l_add is None else l_add + e
                pr.append(e.astype(BF16))
            pv = jnp.dot(jnp.concatenate(pr, axis=1), v, preferred_element_type=F32)
            l_ref[mi, rows, :] = alpha * l_ref[mi, rows, :] + l_add
            acc_ref[mi, rows, :] = jnp.concatenate([alpha] * (dv // LANE), axis=1) * acc_ref[mi, rows, :] + pv
            m_ref[mi, rows, :] = m_new

    @pl.when(ki < qi)
    def _below_diagonal():
        for r in range(tq // sub):
            update(r, False)

    @pl.when(ki == qi)
    def _diagonal():
        lam = _lambda(lq1_ref, lk1_ref, lq2_ref, lk2_ref, lam_init)
        for r in range(tq // sub):
            update(r, True)
            rows = slice(r * sub, (r + 1) * sub)
            l1 = jnp.sum(l_ref[0, rows, :], axis=-1, keepdims=True)
            l2 = jnp.sum(l_ref[1, rows, :], axis=-1, keepdims=True)
            o = acc_ref[0, rows, :] / l1 - lam * (acc_ref[1, rows, :] / l2)
            o_ref[rows, :] = (_rms(o, gsub_ref[...]) * (1.0 - lam_init)).astype(o_ref.dtype)


def _diff_attention_prompt(q, k, v, lams, g_sub, lam_init, batch, seq, out_dtype):
    n, width = q.shape
    dv = width // N_DIFF_HEADS
    dh = dv // 2
    tq = _tile(seq, 1024)
    sub = _tile(tq, 128)
    assert sub % LANE == 0 or tq == sub
    nq = seq // tq
    pairs = [(a, b) for a in range(nq) for b in range(a + 1)]
    qi = jnp.asarray([a for a, _ in pairs], I32)
    ki = jnp.asarray([b for _, b in pairs], I32)
    grid_spec = pltpu.PrefetchScalarGridSpec(
        num_scalar_prefetch=2,
        grid=(batch, N_DIFF_HEADS, len(pairs)),
        in_specs=[
            pl.BlockSpec((tq, dv), lambda b, h, p, qi, ki: (b * nq + qi[p], h)),
            pl.BlockSpec((tq, dv), lambda b, h, p, qi, ki: (b * nq + ki[p], h)),
            pl.BlockSpec((tq, dv), lambda b, h, p, qi, ki: (b * nq + ki[p], h)),
        ] + [pl.BlockSpec((1, dh), lambda b, h, p, qi, ki: (0, 0))] * 4
          + [pl.BlockSpec((1, dv), lambda b, h, p, qi, ki: (0, 0))],
        out_specs=pl.BlockSpec((tq, dv), lambda b, h, p, qi, ki: (b * nq + qi[p], h)),
        scratch_shapes=[pltpu.VMEM((2, tq, LANE), F32), pltpu.VMEM((2, tq, LANE), F32), pltpu.VMEM((2, tq, dv), F32)],
    )
    return pl.pallas_call(
        functools.partial(_diff_prompt_kernel, dh=dh, sub=sub, lam_init=lam_init),
        grid_spec=grid_spec,
        out_shape=jax.ShapeDtypeStruct((n, width), out_dtype),
        compiler_params=_cparams(("parallel", "parallel", "arbitrary"), 40),
        name="diff_attention_prompt",
    )(qi, ki, q, k, v, *lams, g_sub)


def _diff_decode_kernel(pt_ref, q_ref, kn_ref, vn_ref, lq1_ref, lk1_ref, lq2_ref, lk2_ref, gsub_ref, *rest,
                        n_pg, t_new, dh, lam_init):
    k_refs = rest[:n_pg]
    v_refs = rest[n_pg:2 * n_pg]
    o_ref, m_ref, l_ref, acc_ref, qall_ref = rest[2 * n_pg:]
    g = pl.program_id(1)
    n_heads = N_DIFF_HEADS
    dv = 2 * dh
    rows = 2 * t_new
    page = k_refs[0].shape[0]
    kv_rows = page * n_heads
    row_q = lax.broadcasted_iota(I32, (rows, dv), 0)
    lane_q = lax.broadcasted_iota(I32, (rows, dv), 1)

    own_head = ((lax.broadcasted_iota(I32, (n_heads * rows, LANE), 1) & (n_heads - 1))
                == lax.broadcasted_iota(I32, (n_heads * rows, LANE), 0) >> (rows.bit_length() - 1))

    def scores(k2, keep):
        s = lax.dot_general(qall_ref[...], k2, _NT, preferred_element_type=F32)
        return [jnp.where(keep, s[:, c * LANE:(c + 1) * LANE], -jnp.inf) for c in range(k2.shape[0] // LANE)]

    def accumulate(blocks, v_list, first):
        m_cur = blocks[0]
        for blk in blocks[1:]:
            m_cur = jnp.maximum(m_cur, blk)
        m_new = jnp.max(m_cur, axis=-1, keepdims=True)
        if not first:
            m_prev = m_ref[...]
            m_new = jnp.maximum(m_prev, m_new)
        pr = [jnp.exp2(blk - m_new) for blk in blocks]
        l_add = pr[0]
        for e in pr[1:]:
            l_add = l_add + e
        l_add = jnp.sum(l_add, axis=-1, keepdims=True)
        per_v = len(blocks) // len(v_list)
        pv = None
        for jj, vv in enumerate(v_list):
            part = jnp.dot(jnp.concatenate(pr[jj * per_v:(jj + 1) * per_v], axis=1), vv, preferred_element_type=F32)
            pv = part if pv is None else pv + part
        if first:
            l_ref[...] = l_add
            acc_ref[...] = pv
        else:
            alpha = jnp.exp2(m_prev - m_new)
            l_ref[...] = alpha * l_ref[...] + l_add
            acc_ref[...] = alpha * acc_ref[...] + pv
        m_ref[...] = m_new

    @pl.when(g == 0)
    def _new_tokens():
        for h in range(n_heads):
            q = q_ref[0, :, h * dv:(h + 1) * dv].astype(F32)
            q_up = jnp.where((row_q < t_new) & (lane_q < dh), q, 0.0)
            q_dn = jnp.where((row_q >= t_new) & (lane_q >= dh), pltpu.roll(q, t_new, 0), 0.0)
            qall_ref[h * rows:(h + 1) * rows, :] = q_up + q_dn
        pad = jnp.zeros((LANE - rows * n_heads, dv), F32)
        kn = jnp.concatenate([kn_ref[0].reshape(rows * n_heads, dv), pad], axis=0)
        vn = jnp.concatenate([vn_ref[0].reshape(rows * n_heads, dv), pad], axis=0)
        r = lax.broadcasted_iota(I32, (n_heads * rows, LANE), 0) & (rows - 1)
        tok_q = jnp.where(r >= t_new, r - t_new, r)
        tok_k = lax.broadcasted_iota(I32, (n_heads * rows, LANE), 1) >> (n_heads.bit_length() - 1)
        accumulate(scores(kn, own_head & (tok_k <= tok_q)), [vn], True)

    blocks = []
    for kr in k_refs:
        blocks += scores(kr[...].reshape(kv_rows, dv), own_head)
    accumulate(blocks, [vr[...].reshape(kv_rows, dv) for vr in v_refs], False)

    @pl.when(g == pl.num_programs(1) - 1)
    def _finish():
        lam = _lambda(lq1_ref, lk1_ref, lq2_ref, lk2_ref, lam_init)
        a_all = acc_ref[...] / l_ref[...]
        for h in range(n_heads):
            a = a_all[h * rows:(h + 1) * rows]
            o = a - lam * pltpu.roll(a, t_new, 0)
            o = _rms(o, gsub_ref[...]) * (1.0 - lam_init)
            o_ref[0, :, h * dv:(h + 1) * dv] = jnp.where(row_q < t_new, o, 0.0)


def _diff_attention_decode(q, k_new, v_new, cache_k, cache_v, layer, page_table, lams, g_sub, lam_init, t_new):
    batch, rows, width = q.shape
    assert rows == 2 * t_new == SUBLANE, "decode kernel stacks the two softmax maps of the new tokens in one sublane tile"
    dv = width // N_DIFF_HEADS
    dh = dv // 2
    n_pages = page_table.shape[1]
    page = cache_k.shape[2]
    n_pg = next(c for c in (16, 8, 4, 2, 1) if n_pages % c == 0)
    steps = n_pages // n_pg
    seq_blk = lambda b, g, pt: (b, 0, 0)
    seq_blk4 = lambda b, g, pt: (b, 0, 0, 0)
    fixed = lambda b, g, pt: (0, 0)

    def page_spec(jj):
        return pl.BlockSpec((None, None, page, N_DIFF_HEADS, dv),
                            lambda b, g, pt: (layer, pt[b * n_pages + g * n_pg + jj], 0, 0, 0))

    grid_spec = pltpu.PrefetchScalarGridSpec(
        num_scalar_prefetch=1,
        grid=(batch, steps),
        in_specs=[pl.BlockSpec((1, rows, width), seq_blk)]
                 + [pl.BlockSpec((1, rows, N_DIFF_HEADS, dv), seq_blk4)] * 2
                 + [pl.BlockSpec((1, dh), fixed)] * 4 + [pl.BlockSpec((1, dv), fixed)]
                 + [page_spec(jj) for jj in range(n_pg)] * 2,
        out_specs=pl.BlockSpec((1, rows, width), seq_blk),
        scratch_shapes=[pltpu.VMEM((N_DIFF_HEADS * rows, 1), F32), pltpu.VMEM((N_DIFF_HEADS * rows, 1), F32),
                        pltpu.VMEM((N_DIFF_HEADS * rows, dv), F32), pltpu.VMEM((N_DIFF_HEADS * rows, dv), F32)],
    )
    return pl.pallas_call(
        functools.partial(_diff_decode_kernel, n_pg=n_pg, t_new=t_new, dh=dh, lam_init=lam_init),
        grid_spec=grid_spec,
        out_shape=jax.ShapeDtypeStruct((batch, rows, width), F32),
        compiler_params=_cparams(("parallel", "arbitrary"), 48),
        name="diff_attention_decode",
    )(page_table.reshape(-1), q, k_new, v_new, *lams, g_sub, *([cache_k] * n_pg), *([cache_v] * n_pg))


def _gla_chunk(q, k, v, g, s_t, chunk):
    nb = chunk // GLA_SUB
    row1 = lax.broadcasted_iota(I32, (chunk, 1), 0)
    b = g
    step = 1
    while step < chunk:
        b = b + jnp.where(row1 >= step, pltpu.roll(b, step, 0), 0.0)
        step *= 2
    b_last = b[chunk - 1:chunk, :]

    qe = (q * jnp.exp(b)).astype(BF16)
    o = lax.dot_general(qe, s_t.astype(BF16), _NT, preferred_element_type=F32)
    kd = (k * jnp.exp(b_last - b)).astype(BF16)
    s_new = s_t * jnp.exp(b_last) + lax.dot_general(v.astype(BF16), kd, _TN, preferred_element_type=F32)

    row = lax.broadcasted_iota(I32, (chunk, chunk), 0)
    col = lax.broadcasted_iota(I32, (chunk, chunk), 1)
    row_in_blk = row1 & (GLA_SUB - 1)

    att = jnp.where(col == row, jnp.sum(q * k, axis=-1, keepdims=True), 0.0)
    for d in range(1, min(GLA_SUB, chunk)):
        k_sh = pltpu.roll(k, d, 0)
        b_sh = pltpu.roll(b, d, 0)
        decay = jnp.exp(jnp.where(row_in_blk >= d, b - b_sh, -jnp.inf))
        w = jnp.sum(q * k_sh * decay, axis=-1, keepdims=True)
        att = att + jnp.where(col == row - d, w, 0.0)

    dk = q.shape[1]
    for j in range(nb - 1):
        r0 = (j + 1) * GLA_SUB
        b_end = b[r0 - 1:r0, :]
        kj = k[r0 - GLA_SUB:r0] * jnp.exp(b_end - b[r0 - GLA_SUB:r0])
        above = [jnp.zeros((r0 - GLA_SUB, dk), F32)] if j else []
        kj = jnp.concatenate(above + [kj, jnp.zeros((chunk - r0, dk), F32)], axis=0)
        qj = (q[r0:] * jnp.exp(b[r0:] - b_end)).astype(BF16)
        a = lax.dot_general(qj, kj.astype(BF16), _NT, preferred_element_type=F32)
        att = att + jnp.concatenate([jnp.zeros((r0, chunk), F32), a], axis=0)

    o = o + jnp.dot(att.astype(BF16), v.astype(BF16), preferred_element_type=F32)
    return o, s_new


def _gla_kernel(gqk_ref, gv_ref, gr_ref, glog_ref, s0_ref, ggla_ref, o_ref, s_ref, st_ref,
                *, chunk, t_valid, dk, dv, scale):
    tb = pl.program_id(1)
    t_blk = gqk_ref.shape[1]
    n_heads = N_GLA_HEADS

    @pl.when(tb == 0)
    def _load_state():
        for h in range(n_heads):
            st_ref[h] = s0_ref[0, h].T

    g_gla = ggla_ref[...]

    def body(c, carry):
        r0 = pl.multiple_of(c * chunk, chunk)
        rows = pl.ds(r0, chunk)
        pos = tb * t_blk + r0 + lax.broadcasted_iota(I32, (chunk, 1), 0)
        live = pos < t_valid
        for h in range(n_heads):
            q = gqk_ref[0, rows, h * dk:(h + 1) * dk] * scale
            k = gqk_ref[0, rows, (n_heads + h) * dk:(n_heads + h + 1) * dk]
            v = gv_ref[0, rows, h * dv:(h + 1) * dv]
            g = jnp.where(live, glog_ref[0, rows, h * dk:(h + 1) * dk], 0.0)
            k = jnp.where(live, k, 0.0)
            o, s_new = _gla_chunk(q, k, v, g, st_ref[h], chunk)
            st_ref[h] = s_new
            gr = gr_ref[0, rows, h * dv:(h + 1) * dv]
            o_ref[0, rows, h * dv:(h + 1) * dv] = (_rms(o, g_gla) * (gr * _sigmoid(gr))).astype(o_ref.dtype)
        return carry

    lax.fori_loop(0, t_blk // chunk, body, 0)

    @pl.when(tb == pl.num_programs(1) - 1)
    def _store_state():
        for h in range(n_heads):
            s_ref[0, h] = st_ref[h].T


def _gla(gqk, gv, gr, glog, s0, g_gla, t_valid, out_dtype):
    batch, seq, _ = gqk.shape
    dk, dv = s0.shape[2], s0.shape[3]
    chunk = next(c for c in (64, 32, 16, 8) if seq % c == 0)
    t_blk = _tile(seq, 256)
    tok = lambda b, t: (b, t, 0)
    return pl.pallas_call(
        functools.partial(_gla_kernel, chunk=chunk, t_valid=t_valid, dk=dk, dv=dv, scale=dk ** -0.5),
        grid=(batch, seq // t_blk),
        in_specs=[
            pl.BlockSpec((1, t_blk, gqk.shape[2]), tok),
            pl.BlockSpec((1, t_blk, gv.shape[2]), tok),
            pl.BlockSpec((1, t_blk, gr.shape[2]), tok),
            pl.BlockSpec((1, t_blk, glog.shape[2]), tok),
            pl.BlockSpec((1, N_GLA_HEADS, dk, dv), lambda b, t: (b, 0, 0, 0)),
            pl.BlockSpec((1, dv), lambda b, t: (0, 0)),
        ],
        out_specs=[
            pl.BlockSpec((1, t_blk, gv.shape[2]), tok),
            pl.BlockSpec((1, N_GLA_HEADS, dk, dv), lambda b, t: (b, 0, 0, 0)),
        ],
        out_shape=[jax.ShapeDtypeStruct(gv.shape, out_dtype), jax.ShapeDtypeStruct(s0.shape, F32)],
        scratch_shapes=[pltpu.VMEM((N_GLA_HEADS, dv, dk), F32)],
        compiler_params=_cparams(("parallel", "arbitrary"), 32),
        name="gla",
    )(gqk, gv, gr, glog, s0, g_gla)


def _cross_kernel(q_ref, mk_ref, mv_ref, o_ref, kh_ref, vh_ref, *, dh):
    @pl.when(pl.program_id(1) == 0)
    def _split_heads():
        for h in range(N_MEM_HEADS):
            kh_ref[h] = mk_ref[0, :, h, :].astype(BF16)
            vh_ref[h] = mv_ref[0, :, h, :].astype(BF16)

    for h in range(N_MEM_HEADS):
        sl = slice(h * dh, (h + 1) * dh)
        s = lax.dot_general(q_ref[0, :, sl].astype(BF16), kh_ref[h], _NT, preferred_element_type=F32)
        pr = jnp.exp(s - jnp.max(s, axis=-1, keepdims=True))
        pr = pr / jnp.sum(pr, axis=-1, keepdims=True)
        o_ref[0, :, sl] = jnp.dot(pr.astype(BF16), vh_ref[h], preferred_element_type=F32).astype(o_ref.dtype)


def _cross_few_rows_kernel(q_ref, mk_ref, mv_ref, o_ref, *, dh):
    rows = q_ref.shape[1]
    mem = mk_ref.shape[1]
    q = jnp.concatenate([q_ref[0, :, h * dh:(h + 1) * dh].astype(F32) for h in range(N_MEM_HEADS)], axis=0)
    k2 = mk_ref[0].reshape(mem * N_MEM_HEADS, dh)
    v2 = mv_ref[0].reshape(mem * N_MEM_HEADS, dh)
    s = lax.dot_general(q, k2, _NT, preferred_element_type=F32)
    own_head = ((lax.broadcasted_iota(I32, s.shape, 1) & (N_MEM_HEADS - 1))
                == lax.broadcasted_iota(I32, s.shape, 0) >> (rows.bit_length() - 1))
    s = jnp.where(own_head, s, -jnp.inf)
    pr = jnp.exp(s - jnp.max(s, axis=-1, keepdims=True))
    pr = pr / jnp.sum(pr, axis=-1, keepdims=True)
    o = jnp.dot(pr, v2, preferred_element_type=F32)
    for h in range(N_MEM_HEADS):
        o_ref[0, :, h * dh:(h + 1) * dh] = o[h * rows:(h + 1) * rows].astype(o_ref.dtype)


def _cross_attention(q, mk, mv, mem_index):
    batch, seq, d = q.shape
    mem, n_heads, dh = mk.shape[-3:]
    tq = _tile(seq, 512)
    lead = mk.ndim - 4
    mem_spec = pl.BlockSpec((None,) * lead + (1, mem, n_heads, dh), lambda b, t: mem_index(b) + (0, 0, 0))
    if seq == SUBLANE:
        return pl.pallas_call(
            functools.partial(_cross_few_rows_kernel, dh=dh),
            grid=(batch, 1),
            in_specs=[pl.BlockSpec((1, seq, d), lambda b, t: (b, 0, 0)), mem_spec, mem_spec],
            out_specs=pl.BlockSpec((1, seq, d), lambda b, t: (b, 0, 0)),
            out_shape=jax.ShapeDtypeStruct(q.shape, F32),
            compiler_params=_cparams(("parallel", "arbitrary"), 40),
            name="cross_attention_few_rows",
        )(q, mk, mv)
    return pl.pallas_call(
        functools.partial(_cross_kernel, dh=dh),
        grid=(batch, seq // tq),
        in_specs=[pl.BlockSpec((1, tq, d), lambda b, t: (b, t, 0)), mem_spec, mem_spec],
        out_specs=pl.BlockSpec((1, tq, d), lambda b, t: (b, t, 0)),
        out_shape=jax.ShapeDtypeStruct(q.shape, BF16 if tq % 16 == 0 else F32),
        scratch_shapes=[pltpu.VMEM((n_heads, mem, dh), BF16), pltpu.VMEM((n_heads, mem, dh), BF16)],
        compiler_params=_cparams(("parallel", "arbitrary"), 40),
        name="cross_attention",
    )(q, mk, mv)


def _two_group_specs(n_a, n_b, tm, d):
    ta = n_a // tm
    return (pl.BlockSpec((tm, d), lambda i, *_: (jnp.minimum(i, ta - 1), 0)),
            pl.BlockSpec((tm, d), lambda i, *_: (jnp.maximum(i - ta, 0), 0)))


def _route_kernel(xa_ref, xb_ref, g_ref, wh_ref, wl_ref, xn_ref, idx_ref, wgt_ref, *, tiles_a):
    x = jnp.where(pl.program_id(0) < tiles_a, xa_ref[...], xb_ref[...])
    xn = _rms(x, g_ref[...])
    xn_ref[...] = xn
    xh = xn.astype(BF16)
    xl = (xn - xh.astype(F32)).astype(BF16)
    logits = (jnp.dot(xh, wh_ref[...], preferred_element_type=F32)
              + jnp.dot(xl, wh_ref[...], preferred_element_type=F32)
              + jnp.dot(xh, wl_ref[...], preferred_element_type=F32))
    lane = lax.broadcasted_iota(I32, logits.shape, 1)
    lane_f = lane.astype(F32)
    neg = -jnp.inf

    def first_max(vals):
        top = jnp.max(vals, axis=-1, keepdims=True)
        first = jnp.min(jnp.where(vals == top, lane_f, float(LANE)), axis=-1, keepdims=True)
        return top, first.astype(I32)

    is_grp = lane < N_GROUPS
    g_top, g_idx = first_max(jnp.where(is_grp, logits, neg))
    g_w = 1.0 / jnp.sum(jnp.where(is_grp, jnp.exp(logits - g_top), 0.0), axis=-1, keepdims=True)
    lo = N_GROUPS + g_idx * EXPERTS_PER_GROUP
    le = jnp.where((lane >= lo) & (lane < lo + EXPERTS_PER_GROUP), logits, neg)
    v1, i1 = first_max(le)
    v2, i2 = first_max(jnp.where(lane == i1, neg, le))
    e2 = jnp.exp(v2 - v1)
    w1 = g_w / (1.0 + e2)
    w2 = g_w * e2 / (1.0 + e2)
    idx_ref[...] = jnp.where(lane == 0, i1 - N_GROUPS, jnp.where(lane == 1, i2 - N_GROUPS, 0))
    wgt_ref[...] = jnp.where(lane == 0, w1, jnp.where(lane == 1, w2, 0.0))


def _route(xa, xb, tm, g_ffn, w_hi, w_lo):
    d = xa.shape[1]
    n = xa.shape[0] + xb.shape[0]
    row = lambda i: (i, 0)
    fixed = lambda i: (0, 0)
    return pl.pallas_call(
        functools.partial(_route_kernel, tiles_a=xa.shape[0] // tm),
        grid=(n // tm,),
        in_specs=[*_two_group_specs(xa.shape[0], xb.shape[0], tm, d), pl.BlockSpec((1, d), fixed),
                  pl.BlockSpec((d, LANE), fixed), pl.BlockSpec((d, LANE), fixed)],
        out_specs=[pl.BlockSpec((tm, d), row), pl.BlockSpec((tm, LANE), row), pl.BlockSpec((tm, LANE), row)],
        out_shape=[jax.ShapeDtypeStruct((n, d), F32), jax.ShapeDtypeStruct((n, LANE), I32),
                   jax.ShapeDtypeStruct((n, LANE), F32)],
        compiler_params=_cparams(("parallel",), 40),
        name="moe_route",
    )(xa, xb, g_ffn, w_hi, w_lo)


def _start_row_copies(n_rows, make_copy):
    def start(it, c):
        for u in range(DMA_UNROLL):
            for kk in range(2):
                make_copy(it * DMA_UNROLL + u, kk).start(priority=kk)
        return c

    assert n_rows % DMA_UNROLL == 0
    lax.fori_loop(0, n_rows // DMA_UNROLL, start, 0)


def _wait_row_copies(n_rows, make_copy):
    def wait(it, c):
        for u in range(DMA_UNROLL):
            for kk in range(2):
                make_copy(it * DMA_UNROLL + u, kk).wait()
        return c

    lax.fori_loop(0, n_rows // DMA_UNROLL, wait, 0)


def _dispatch_kernel(pos_ref, pend_ref, x_ref, xs_ref, zero_ref, sem, zsem, *, tile_e, tm):
    i = pl.program_id(0)

    @pl.when(i == 0)
    def _zero_padding():
        zero_ref[...] = jnp.zeros(zero_ref.shape, F32)

        def fill(e):
            start = pl.multiple_of(jnp.maximum(pend_ref[e] - tile_e, 0), tile_e)
            return pltpu.make_async_copy(zero_ref, xs_ref.at[pl.ds(start, tile_e), :], zsem)

        for e in range(N_EXPERTS):
            fill(e).start()
        for e in range(N_EXPERTS):
            fill(e).wait()

        def tail(t):
            return pltpu.make_async_copy(zero_ref, xs_ref.at[pl.ds(pl.multiple_of(t * tile_e, tile_e), tile_e), :], zsem)

        first_unused = pend_ref[N_EXPERTS - 1] // tile_e
        n_tiles = xs_ref.shape[0] // tile_e
        lax.fori_loop(first_unused, n_tiles, lambda t, c: (tail(t).start(), c)[1], 0)
        lax.fori_loop(first_unused, n_tiles, lambda t, c: (tail(t).wait(), c)[1], 0)

    def row_copy(r, kk):
        dst = pos_ref[(i * tm + r) * 2 + kk]
        return pltpu.make_async_copy(x_ref.at[pl.ds(r, 1), :], xs_ref.at[pl.ds(dst, 1), :], sem.at[kk])

    _start_row_copies(tm, row_copy)
    _wait_row_copies(tm, row_copy)


def _dispatch(xn, tm, pos, p_end, n_slots, tile_e):
    n, d = xn.shape
    grid_spec = pltpu.PrefetchScalarGridSpec(
        num_scalar_prefetch=2,
        grid=(n // tm,),
        in_specs=[pl.BlockSpec((tm, d), lambda i, pos, pend: (i, 0))],
        out_specs=pl.BlockSpec(memory_space=pl.ANY),
        scratch_shapes=[pltpu.VMEM((tile_e, d), F32), pltpu.SemaphoreType.DMA((2,)), pltpu.SemaphoreType.DMA(())],
    )
    return pl.pallas_call(
        functools.partial(_dispatch_kernel, tile_e=tile_e, tm=tm),
        grid_spec=grid_spec,
        out_shape=jax.ShapeDtypeStruct((n_slots, d), F32),
        compiler_params=_cparams(("arbitrary",), 32),
        name="moe_dispatch",
    )(pos, p_end, xn)


def _experts_kernel(te_ref, nu_ref, first_ref, next_ref, slot_ref, x_ref, wg_hbm, wu_hbm, wd_hbm, y_ref,
                    wgf_ref, wuf_ref, wdf_ref, wgb_ref, wub_ref, wdb_ref, sem):
    t = pl.program_id(0)

    def fetch(e, slot):
        return [pltpu.make_async_copy(src.at[0, e], dst.at[slot], sem.at[slot, j])
                for j, (src, dst) in enumerate(((wg_hbm, wgf_ref), (wu_hbm, wuf_ref), (wd_hbm, wdf_ref)))]

    @pl.when(t == 0)
    def _first_fetch():
        for c in fetch(te_ref[0], slot_ref[0]):
            c.start()

    @pl.when(first_ref[t] == 1)
    def _new_expert():
        slot = slot_ref[t]
        for c in fetch(te_ref[t], slot):
            c.wait()

        @pl.when(next_ref[t] >= 0)
        def _prefetch():
            for c in fetch(next_ref[t], 1 - slot):
                c.start()

        wgb_ref[...] = wgf_ref[slot].astype(BF16)
        wub_ref[...] = wuf_ref[slot].astype(BF16)
        wdb_ref[...] = wdf_ref[slot].astype(BF16)

    @pl.when(t < nu_ref[0])
    def _mlp():
        x = x_ref[...].astype(BF16)
        hg = jnp.dot(x, wgb_ref[...], preferred_element_type=F32)
        hu = jnp.dot(x, wub_ref[...], preferred_element_type=F32)
        h = (hg * _sigmoid(hg)) * hu
        y_ref[...] = jnp.dot(h.astype(BF16), wdb_ref[...], preferred_element_type=F32)

    @pl.when(t >= nu_ref[0])
    def _unused():
        y_ref[...] = jnp.zeros(y_ref.shape, F32)


def _experts(xs, tile_expert, n_used, w_gate, w_up, w_down, tm):
    n_slots, d = xs.shape
    de = w_gate.shape[-1]
    assert w_gate.ndim == 4 and w_gate.shape[0] == 1, "weights come as (1, experts, rows, cols)"
    n_tiles = n_slots // tm
    idx = jnp.arange(n_tiles, dtype=I32)
    first = jnp.concatenate([jnp.ones((1,), I32), (tile_expert[1:] != tile_expert[:-1]).astype(I32)])
    slot = (jnp.cumsum(first) - 1) & 1
    run_start = jnp.where(first == 1, idx, n_tiles)
    next_start = jnp.concatenate([lax.cummin(run_start, reverse=True)[1:], jnp.full((1,), n_tiles, I32)])
    next_expert = jnp.where(next_start < n_tiles, tile_expert[jnp.minimum(next_start, n_tiles - 1)], -1)
    any_spec = pl.BlockSpec(memory_space=pl.ANY)
    grid_spec = pltpu.PrefetchScalarGridSpec(
        num_scalar_prefetch=5,
        grid=(n_tiles,),
        in_specs=[pl.BlockSpec((tm, d), lambda t, te, nu, *_: (jnp.minimum(t, nu[0] - 1), 0)),
                  any_spec, any_spec, any_spec],
        out_specs=pl.BlockSpec((tm, d), lambda t, *_: (t, 0)),
        scratch_shapes=[pltpu.VMEM((2, d, de), F32), pltpu.VMEM((2, d, de), F32), pltpu.VMEM((2, de, d), F32),
                        pltpu.VMEM((d, de), BF16), pltpu.VMEM((d, de), BF16), pltpu.VMEM((de, d), BF16),
                        pltpu.SemaphoreType.DMA((2, 3))],
    )
    return pl.pallas_call(
        _experts_kernel,
        grid_spec=grid_spec,
        out_shape=jax.ShapeDtypeStruct((n_slots, d), F32),
        compiler_params=_cparams(("arbitrary",), 52),
        name="moe_experts",
    )(tile_expert, n_used, first, next_expert.astype(I32), slot.astype(I32), xs, w_gate, w_up, w_down)


def _combine_kernel(pos_ref, xa_ref, xb_ref, wgt_ref, g_ref, ys_ref, oa_ref, ob_ref, buf_ref, sem, *, tiles_a):
    i = pl.program_id(0)
    tm = xa_ref.shape[0]

    def gather(step):
        slot = step & 1

        def row_copy(r, kk):
            src = pos_ref[(step * tm + r) * 2 + kk]
            return pltpu.make_async_copy(ys_ref.at[pl.ds(src, 1), :], buf_ref.at[slot, kk, pl.ds(r, 1), :],
                                         sem.at[slot, kk])
        return row_copy

    @pl.when(i == 0)
    def _():
        _start_row_copies(tm, gather(i))

    @pl.when(i + 1 < pl.num_programs(0))
    def _():
        _start_row_copies(tm, gather(i + 1))

    _wait_row_copies(tm, gather(i))
    slot = i & 1
    wgt = wgt_ref[...]
    first_group = i < tiles_a
    x = jnp.where(first_group, xa_ref[...], xb_ref[...])
    y = _rms(x + (wgt[:, 0:1] * buf_ref[slot, 0] + wgt[:, 1:2] * buf_ref[slot, 1]), g_ref[...])

    @pl.when(first_group)
    def _():
        oa_ref[...] = y

    @pl.when(jnp.logical_not(first_group))
    def _():
        ob_ref[...] = y


def _combine(xa, xb, tm, wgt, g_final, ys, pos):
    d = xa.shape[1]
    n = xa.shape[0] + xb.shape[0]
    grid_spec = pltpu.PrefetchScalarGridSpec(
        num_scalar_prefetch=1,
        grid=(n // tm,),
        in_specs=[*_two_group_specs(xa.shape[0], xb.shape[0], tm, d), pl.BlockSpec((tm, LANE), lambda i, pos: (i, 0)),
                  pl.BlockSpec((1, d), lambda i, pos: (0, 0)), pl.BlockSpec(memory_space=pl.ANY)],
        out_specs=list(_two_group_specs(xa.shape[0], xb.shape[0], tm, d)),
        scratch_shapes=[pltpu.VMEM((2, 2, tm, d), F32), pltpu.SemaphoreType.DMA((2, 2))],
    )
    return pl.pallas_call(
        functools.partial(_combine_kernel, tiles_a=xa.shape[0] // tm),
        grid_spec=grid_spec,
        out_shape=[jax.ShapeDtypeStruct(xa.shape, F32), jax.ShapeDtypeStruct(xb.shape, F32)],
        compiler_params=_cparams(("arbitrary",), 40),
        name="moe_combine",
    )(pos, xa, xb, wgt, g_final, ys)


def _moe_and_final_norm(xa, xb, g_ffn, w_route_hi, w_route_lo, w_gate, w_up, w_down, g_final):
    d = xa.shape[1]
    n = xa.shape[0] + xb.shape[0]
    tm_tok = _tile(math.gcd(xa.shape[0], xb.shape[0]), 256)
    assert tm_tok % DMA_UNROLL == 0
    xn, ridx, rw = _route(xa, xb, tm_tok, g_ffn, w_route_hi, w_route_lo)
    tm = 256 if n >= 2048 else 128
    e = ridx[:, :2].reshape(-1)
    onehot = (e[:, None] == jnp.arange(N_EXPERTS, dtype=I32)[None, :]).astype(I32)
    csum = jnp.cumsum(onehot, axis=0)
    counts = csum[-1]
    padded = (counts + tm - 1) // tm * tm
    p_end = jnp.cumsum(padded).astype(I32)
    pos = jnp.sum(onehot * ((p_end - padded)[None, :] + csum - 1), axis=1).astype(I32)
    n_tiles = (2 * n + N_EXPERTS * (tm - 1)) // tm + 1
    tile_start = jnp.arange(n_tiles, dtype=I32) * tm
    tile_expert = jnp.minimum(jnp.sum((tile_start[:, None] >= p_end[None, :]).astype(I32), axis=1), N_EXPERTS - 1)
    n_used = (p_end[-1:] // tm).astype(I32)
    xs = _dispatch(xn, tm_tok, pos, p_end, n_tiles * tm, tm)
    ys = _experts(xs, tile_expert.astype(I32), n_used, w_gate, w_up, w_down, tm)
    return _combine(xa, xb, tm_tok, rw, g_final, ys, pos)


def _layer(x3, s0, t_valid, mk, mv, mem_index, diff_fn, lw):
    batch, seq, d = x3.shape
    n = batch * seq
    x = x3.reshape(n, d)
    qs, k4, v4, kb, vb, gqk, gv, gr, glog = _in_projection(x, lw["g_mix"], lw["w_in_main"], lw["w_in_ga"],
                                                            lw["w_a2p"], lw["b_a"])
    d_out = diff_fn(qs, k4, v4, kb, vb)
    r3 = lambda a: a.reshape(batch, seq, a.shape[-1])
    g_out, s_new = _gla(r3(gqk), r3(gv), r3(gr), r3(glog), s0, lw["g_gla"], t_valid,
                        BF16 if seq % 16 == 0 else F32)
    x1 = _matmul_residual([d_out.reshape(n, -1), g_out.reshape(n, -1)], lw["w_o"], x, "mixer_out")
    dh_mem = d // N_MEM_HEADS
    xq = _norm_matmul(x1, lw["g_cross"], lw["w_q_mem"], BF16 if seq % 16 == 0 else F32, "cross_q",
                      out_scale=dh_mem ** -0.5)
    ca = _cross_attention(xq.reshape(batch, seq, d), mk, mv, mem_index)
    x2 = _matmul_residual([ca.reshape(n, d)], lw["w_o_mem"], x1, "cross_out")
    return x2, k4, v4, s_new


def kernel(x_prompt, x_sample, mem_prompt, cache_k, cache_v, state_gla, cache_mem_k, cache_mem_v, page_table, g_mix, w_in, w_a2, b_a, lambda_q1, lambda_k1, lambda_q2, lambda_k2, g_subln, g_gla, w_o, g_mem, g_cross, w_q_mem, w_k_mem, w_v_mem, w_o_mem, g_ffn, w_group, w_router, w_gate, w_up, w_down, g_final):
    depth = w_in.shape[0]
    batch, seq, d = x_prompt.shape
    dec_batch, dec_seq, _ = x_sample.shape
    mem_len = mem_prompt.shape[1]
    width = d // 2
    main_cols = w_in.shape[2] - GATE_RANK
    assert main_cols == 6 * width and w_a2.shape[1:] == (GATE_RANK, width // 2)
    dk_gla = width // N_GLA_HEADS // 2
    dv_gla = width // N_GLA_HEADS
    dv_diff = width // N_DIFF_HEADS
    dec_pad = -(-dec_seq // SUBLANE) * SUBLANE

    xp = x_prompt
    xs = jnp.pad(x_sample, ((0, 0), (0, dec_pad - dec_seq), (0, 0)))
    s0_prompt = jnp.zeros((batch, N_GLA_HEADS, dk_gla, dv_gla), F32)
    row2 = lambda a: a.reshape(1, -1)
    outs = {name: [] for name in ("kp", "vp", "sp", "mkp", "mvp", "ks", "vs", "ss")}
    for l in range(depth):
        lam_init = 0.8 - 0.6 * math.exp(-0.3 * l)
        lw = dict(
            g_mix=row2(g_mix[l]), g_gla=row2(g_gla[l]), g_cross=row2(g_cross[l]), b_a=row2(b_a[l]),
            w_in_main=w_in[l][:, :main_cols].astype(BF16),
            w_in_ga=jnp.pad(w_in[l][:, main_cols:], ((0, 0), (0, LANE - GATE_RANK))).astype(BF16),
            w_a2p=jnp.pad(w_a2[l], ((0, LANE - GATE_RANK), (0, 0))).astype(BF16),
            w_o=w_o[l].astype(BF16), w_q_mem=w_q_mem[l].astype(BF16), w_o_mem=w_o_mem[l].astype(BF16),
        )
        lams = (row2(lambda_q1[l]), row2(lambda_k1[l]), row2(lambda_q2[l]), row2(lambda_k2[l]))
        g_sub = row2(g_subln[l])
        w_route = jnp.pad(jnp.concatenate([w_group[l], w_router[l]], axis=1),
                          ((0, 0), (0, LANE - N_GROUPS - N_EXPERTS)))
        w_route_hi = w_route.astype(BF16)
        w_route_lo = (w_route - w_route_hi.astype(F32)).astype(BF16)

        mem2 = mem_prompt.reshape(batch * mem_len, d)
        mkp = _norm_matmul(mem2, row2(g_mem[l]), w_k_mem[l].astype(BF16), F32, "mem_k", n_heads=N_MEM_HEADS)
        mvp = _norm_matmul(mem2, row2(g_mem[l]), w_v_mem[l].astype(BF16), F32, "mem_v", n_heads=N_MEM_HEADS)
        mkp = mkp.reshape(batch, mem_len, N_MEM_HEADS, -1)
        mvp = mvp.reshape(batch, mem_len, N_MEM_HEADS, -1)

        prompt_attn = lambda qs, k4, v4, kb, vb: _diff_attention_prompt(qs, kb, vb, lams, g_sub, lam_init, batch, seq,
                                                                        BF16)
        xp2, kp, vp, sp = _layer(xp, s0_prompt, seq, mkp, mvp, lambda b: (b,), prompt_attn, lw)

        sample_attn = lambda qs, k4, v4, kb, vb, l=l: _diff_attention_decode(
            qs.reshape(dec_batch, dec_pad, width), k4.reshape(dec_batch, dec_pad, N_DIFF_HEADS, dv_diff),
            v4.reshape(dec_batch, dec_pad, N_DIFF_HEADS, dv_diff), cache_k, cache_v, l, page_table, lams, g_sub,
            lam_init, dec_seq)
        xs2, ks, vs, ss = _layer(xs, state_gla[l], dec_seq, cache_mem_k, cache_mem_v, lambda b, l=l: (l, b),
                                 sample_attn, lw)

        assert l == depth - 1, "the MoE combine kernel fuses the final norm; deeper stacks need an un-normed variant"
        moe = functools.partial(_moe_and_final_norm, g_ffn=row2(g_ffn[l]), w_route_hi=w_route_hi,
                                w_route_lo=w_route_lo, w_gate=w_gate[l:l + 1], w_up=w_up[l:l + 1],
                                w_down=w_down[l:l + 1], g_final=row2(g_final))
        xp, xs = moe(xp2, xs2)
        xp = xp.reshape(batch, seq, d)
        xs = xs.reshape(dec_batch, dec_pad, d)

        outs["kp"].append(kp.reshape(batch, seq, N_DIFF_HEADS, dv_diff))
        outs["vp"].append(vp.reshape(batch, seq, N_DIFF_HEADS, dv_diff))
        outs["sp"].append(sp)
        outs["mkp"].append(mkp)
        outs["mvp"].append(mvp)
        outs["ks"].append(ks.reshape(dec_batch, dec_pad, N_DIFF_HEADS, dv_diff)[:, :dec_seq])
        outs["vs"].append(vs.reshape(dec_batch, dec_pad, N_DIFF_HEADS, dv_diff)[:, :dec_seq])
        outs["ss"].append(ss)

    st = lambda name: jnp.stack(outs[name])
    return (xp, xs[:, :dec_seq], st("kp"), st("vp"), st("sp"), st("mkp"), st("mvp"), st("ks"), st("vs"), st("ss"))
```

```python
import functools
import math

import jax
import jax.numpy as jnp
from jax import lax
from jax.experimental import pallas as pl
from jax.experimental.pallas import tpu as pltpu

F32 = jnp.float32
BF16 = jnp.bfloat16
I32 = jnp.int32

N_DIFF_HEADS = 4
N_GLA_HEADS = 4
GATE_RANK = 16
GATE_TAU = 16.0
N_MEM_HEADS = 4
N_GROUPS = 4
EXPERTS_PER_GROUP = 8
N_EXPERTS = N_GROUPS * EXPERTS_PER_GROUP
RMS_EPS = 1e-6
GLA_SUB = 8
LOG2E = math.log2(math.e)

LANE = 128
SUBLANE = 8
MIB = 1024 * 1024
DMA_UNROLL = 8

_NT = (((1,), (1,)), ((), ()))
_TN = (((0,), (0,)), ((), ()))


def _cparams(semantics, vmem_mib):
    return pltpu.CompilerParams(dimension_semantics=semantics, vmem_limit_bytes=int(vmem_mib * MIB))


def _tile(n, target):
    if n <= target:
        return n
    t = target - target % SUBLANE
    while t >= SUBLANE:
        if n % t == 0:
            return t
        t -= SUBLANE
    raise ValueError(f"no sublane-aligned tile of {n} below {target}")


def _rms(x, g):
    ms = jnp.mean(x * x, axis=-1, keepdims=True)
    return x * lax.rsqrt(ms + RMS_EPS) * g


def _sigmoid(x):
    return 1.0 / (1.0 + jnp.exp(-x))


def _resident(a):
    return pl.BlockSpec(a.shape, lambda *_: (0,) * a.ndim, pipeline_mode=pl.Buffered(1))


def _store_heads(o_ref, val, n_heads):
    w = val.shape[1] // n_heads
    for h in range(n_heads):
        o_ref[:, h, :] = val[:, h * w:(h + 1) * w].astype(o_ref.dtype)


def _inproj_kernel(x_ref, g_ref, w_ref, wga_ref, wa2_ref, ba_ref,
                   qs_ref, k_ref, v_ref, kb_ref, vb_ref, gqk_ref, gv_ref, gr_ref, glog_ref, *, q_scale):
    xn = _rms(x_ref[...], g_ref[...]).astype(BF16)
    wblk = qs_ref.shape[1]

    def proj(group):
        return jnp.dot(xn, w_ref[:, group * wblk:(group + 1) * wblk], preferred_element_type=F32)

    qs_ref[...] = (proj(0) * q_scale).astype(BF16)
    for group, (o_ref, ob_ref) in ((1, (k_ref, kb_ref)), (2, (v_ref, vb_ref))):
        r = proj(group)
        _store_heads(o_ref, r, N_DIFF_HEADS)
        ob_ref[...] = r.astype(BF16)
    for group, o_ref in ((3, gqk_ref), (4, gv_ref), (5, gr_ref)):
        o_ref[...] = proj(group)
    ga = jnp.dot(xn, wga_ref[...], preferred_element_type=F32)
    z = jnp.dot(ga.astype(BF16), wa2_ref[...], preferred_element_type=F32) + ba_ref[...]
    glog_ref[...] = (jnp.minimum(z, 0.0) - jnp.log1p(jnp.exp(-jnp.abs(z)))) * (1.0 / GATE_TAU)


def _in_projection(x, g_mix, w_main, w_ga, w_a2p, b_a):
    n, d = x.shape
    wblk = d // 2
    assert w_main.shape[1] == 6 * wblk
    dv = wblk // N_DIFF_HEADS
    gk = w_a2p.shape[1]
    tm = _tile(n, 256)
    row = lambda i: (i, 0)
    row3 = lambda i: (i, 0, 0)
    flat = lambda dt: (jax.ShapeDtypeStruct((n, wblk), dt), pl.BlockSpec((tm, wblk), row))
    heads = (jax.ShapeDtypeStruct((n, N_DIFF_HEADS, dv), F32), pl.BlockSpec((tm, N_DIFF_HEADS, dv), row3))
    outs = [flat(BF16), heads, heads, flat(BF16), flat(BF16), flat(F32), flat(F32), flat(F32),
            (jax.ShapeDtypeStruct((n, gk), F32), pl.BlockSpec((tm, gk), row))]
    return pl.pallas_call(
        functools.partial(_inproj_kernel, q_scale=(dv // 2) ** -0.5 * LOG2E),
        grid=(n // tm,),
        in_specs=[pl.BlockSpec((tm, d), row), _resident(g_mix), _resident(w_main), _resident(w_ga),
                  _resident(w_a2p), _resident(b_a)],
        out_specs=[o[1] for o in outs],
        out_shape=[o[0] for o in outs],
        compiler_params=_cparams(("parallel",), 56),
        name="in_projection",
    )(x, g_mix, w_main, w_ga, w_a2p, b_a)


def _norm_mm_kernel(x_ref, g_ref, w_ref, o_ref, *, out_scale, n_heads):
    xn = _rms(x_ref[...], g_ref[...]).astype(BF16)
    r = jnp.dot(xn, w_ref[...], preferred_element_type=F32)
    if out_scale != 1.0:
        r = r * out_scale
    if n_heads:
        _store_heads(o_ref, r, n_heads)
    else:
        o_ref[...] = r.astype(o_ref.dtype)


def _norm_matmul(x, g, w, out_dtype, name, out_scale=1.0, n_heads=0):
    n, d = x.shape
    dout = w.shape[1]
    tm = _tile(n, 512)
    if n_heads:
        out_spec = pl.BlockSpec((tm, n_heads, dout // n_heads), lambda i: (i, 0, 0))
        out_shape = jax.ShapeDtypeStruct((n, n_heads, dout // n_heads), out_dtype)
    else:
        out_spec = pl.BlockSpec((tm, dout), lambda i: (i, 0))
        out_shape = jax.ShapeDtypeStruct((n, dout), out_dtype)
    return pl.pallas_call(
        functools.partial(_norm_mm_kernel, out_scale=out_scale, n_heads=n_heads),
        grid=(n // tm,),
        in_specs=[pl.BlockSpec((tm, d), lambda i: (i, 0)), _resident(g), _resident(w)],
        out_specs=out_spec,
        out_shape=out_shape,
        compiler_params=_cparams(("parallel",), 48),
        name=name,
    )(x, g, w)


def _mm_res_kernel(*refs, widths):
    a_refs = refs[:len(widths)]
    w_ref, res_ref, o_ref = refs[len(widths):]
    acc = res_ref[...]
    k0 = 0
    for a_ref, kw in zip(a_refs, widths):
        acc = acc + jnp.dot(a_ref[...].astype(BF16), w_ref[k0:k0 + kw, :], preferred_element_type=F32)
        k0 += kw
    o_ref[...] = acc


def _matmul_residual(pieces, w, res, name):
    n, dout = res.shape
    widths = tuple(p.shape[1] for p in pieces)
    assert sum(widths) == w.shape[0]
    tm = _tile(n, 512)
    in_specs = [pl.BlockSpec((tm, kw), lambda i: (i, 0)) for kw in widths]
    in_specs += [_resident(w), pl.BlockSpec((tm, dout), lambda i: (i, 0))]
    return pl.pallas_call(
        functools.partial(_mm_res_kernel, widths=widths),
        grid=(n // tm,),
        in_specs=in_specs,
        out_specs=pl.BlockSpec((tm, dout), lambda i: (i, 0)),
        out_shape=jax.ShapeDtypeStruct((n, dout), F32),
        compiler_params=_cparams(("parallel",), 48),
        name=name,
    )(*pieces, w, res)


def _lambda(lq1_ref, lk1_ref, lq2_ref, lk2_ref, lam_init):
    d1 = jnp.sum(lq1_ref[...] * lk1_ref[...], axis=-1, keepdims=True)
    d2 = jnp.sum(lq2_ref[...] * lk2_ref[...], axis=-1, keepdims=True)
    return jnp.exp(d1) - jnp.exp(d2) + lam_init


def _diff_prompt_kernel(qi_ref, ki_ref, q_ref, k_ref, v_ref, lq1_ref, lk1_ref, lq2_ref, lk2_ref, gsub_ref,
                        o_ref, m_ref, l_ref, acc_ref, *, dh, sub, lam_init):
    p = pl.program_id(2)
    qi = qi_ref[p]
    ki = ki_ref[p]
    tq, tk = q_ref.shape[0], k_ref.shape[0]
    dv = 2 * dh

    @pl.when(ki == 0)
    def _init():
        m_ref[...] = jnp.full(m_ref.shape, -jnp.inf, F32)
        l_ref[...] = jnp.zeros(l_ref.shape, F32)
        acc_ref[...] = jnp.zeros(acc_ref.shape, F32)

    def update(r, diagonal):
        rows = slice(r * sub, (r + 1) * sub)
        ncol = (r + 1) * sub if diagonal else tk
        v = v_ref[0:ncol, :]
        if diagonal:
            keep = lax.broadcasted_iota(I32, (sub, sub), 1) <= lax.broadcasted_iota(I32, (sub, sub), 0)
        for mi in range(2):
            s = lax.dot_general(q_ref[rows, mi * dh:(mi + 1) * dh], k_ref[0:ncol, mi * dh:(mi + 1) * dh], _NT,
                                preferred_element_type=F32)
            blocks = [s[:, c * LANE:(c + 1) * LANE] for c in range(ncol // LANE)]
            if diagonal:
                first = ncol // LANE - sub // LANE
                for c in range(sub // LANE):
                    blocks[first + c] = jnp.where(keep[:, c * LANE:(c + 1) * LANE], blocks[first + c], -jnp.inf)
            m_cur = blocks[0]
            for blk in blocks[1:]:
                m_cur = jnp.maximum(m_cur, blk)
            m_prev = m_ref[mi, rows, :]
            m_new = jnp.maximum(m_prev, jnp.max(m_cur, axis=-1, keepdims=True))
            alpha = jnp.exp2(m_prev - m_new)
            l_add = None
            pr = []
            for blk in blocks:
                e = jnp.exp2(blk - m_new)
                l_add = e if l_add is None else l_add + e
                pr.append(e.astype(BF16))
            pv = jnp.dot(jnp.concatenate(pr, axis=1), v, preferred_element_type=F32)
            l_ref[mi, rows, :] = alpha * l_ref[mi, rows, :] + l_add
            acc_ref[mi, rows, :] = jnp.concatenate([alpha] * (dv // LANE), axis=1) * acc_ref[mi, rows, :] + pv
            m_ref[mi, rows, :] = m_new

    @pl.when(ki < qi)
    def _below_diagonal():
        for r in range(tq // sub):
            update(r, False)

    @pl.when(ki == qi)
    def _diagonal():
        lam = _lambda(lq1_ref, lk1_ref, lq2_ref, lk2_ref, lam_init)
        for r in range(tq // sub):
            update(r, True)
            rows = slice(r * sub, (r + 1) * sub)
            l1 = jnp.sum(l_ref[0, rows, :], axis=-1, keepdims=True)
            l2 = jnp.sum(l_ref[1, rows, :], axis=-1, keepdims=True)
            o = acc_ref[0, rows, :] / l1 - lam * (acc_ref[1, rows, :] / l2)
            o_ref[rows, :] = (_rms(o, gsub_ref[...]) * (1.0 - lam_init)).astype(o_ref.dtype)


def _diff_attention_prompt(q, k, v, lams, g_sub, lam_init, batch, seq, out_dtype):
    n, width = q.shape
    dv = width // N_DIFF_HEADS
    dh = dv // 2
    tq = _tile(seq, 1024)
    sub = _tile(tq, 128)
    assert sub % LANE == 0 or tq == sub
    nq = seq // tq
    pairs = [(a, b) for a in range(nq) for b in range(a + 1)]
    qi = jnp.asarray([a for a, _ in pairs], I32)
    ki = jnp.asarray([b for _, b in pairs], I32)
    grid_spec = pltpu.PrefetchScalarGridSpec(
        num_scalar_prefetch=2,
        grid=(batch, N_DIFF_HEADS, len(pairs)),
        in_specs=[
            pl.BlockSpec((tq, dv), lambda b, h, p, qi, ki: (b * nq + qi[p], h)),
            pl.BlockSpec((tq, dv), lambda b, h, p, qi, ki: (b * nq + ki[p], h)),
            pl.BlockSpec((tq, dv), lambda b, h, p, qi, ki: (b * nq + ki[p], h)),
        ] + [pl.BlockSpec((1, dh), lambda b, h, p, qi, ki: (0, 0))] * 4
          + [pl.BlockSpec((1, dv), lambda b, h, p, qi, ki: (0, 0))],
        out_specs=pl.BlockSpec((tq, dv), lambda b, h, p, qi, ki: (b * nq + qi[p], h)),
        scratch_shapes=[pltpu.VMEM((2, tq, LANE), F32), pltpu.VMEM((2, tq, LANE), F32), pltpu.VMEM((2, tq, dv), F32)],
    )
    return pl.pallas_call(
        functools.partial(_diff_prompt_kernel, dh=dh, sub=sub, lam_init=lam_init),
        grid_spec=grid_spec,
        out_shape=jax.ShapeDtypeStruct((n, width), out_dtype),
        compiler_params=_cparams(("parallel", "parallel", "arbitrary"), 40),
        name="diff_attention_prompt",
    )(qi, ki, q, k, v, *lams, g_sub)


def _diff_decode_kernel(pt_ref, q_ref, kn_ref, vn_ref, lq1_ref, lk1_ref, lq2_ref, lk2_ref, gsub_ref, *rest,
                        n_pg, t_new, dh, lam_init):
    k_refs = rest[:n_pg]
    v_refs = rest[n_pg:2 * n_pg]
    o_ref, m_ref, l_ref, acc_ref, qall_ref = rest[2 * n_pg:]
    g = pl.program_id(1)
    n_heads = N_DIFF_HEADS
    dv = 2 * dh
    rows = 2 * t_new
    page = k_refs[0].shape[0]
    kv_rows = page * n_heads
    row_q = lax.broadcasted_iota(I32, (rows, dv), 0)
    lane_q = lax.broadcasted_iota(I32, (rows, dv), 1)

    own_head = ((lax.broadcasted_iota(I32, (n_heads * rows, LANE), 1) & (n_heads - 1))
                == lax.broadcasted_iota(I32, (n_heads * rows, LANE), 0) >> (rows.bit_length() - 1))

    def scores(k2, keep):
        s = lax.dot_general(qall_ref[...], k2, _NT, preferred_element_type=F32)
        return [jnp.where(keep, s[:, c * LANE:(c + 1) * LANE], -jnp.inf) for c in range(k2.shape[0] // LANE)]

    def accumulate(blocks, v_list, first):
        m_cur = blocks[0]
        for blk in blocks[1:]:
            m_cur = jnp.maximum(m_cur, blk)
        m_new = jnp.max(m_cur, axis=-1, keepdims=True)
        if not first:
            m_prev = m_ref[...]
            m_new = jnp.maximum(m_prev, m_new)
        pr = [jnp.exp2(blk - m_new) for blk in blocks]
        l_add = pr[0]
        for e in pr[1:]:
            l_add = l_add + e
        l_add = jnp.sum(l_add, axis=-1, keepdims=True)
        per_v = len(blocks) // len(v_list)
        pv = None
        for jj, vv in enumerate(v_list):
            part = jnp.dot(jnp.concatenate(pr[jj * per_v:(jj + 1) * per_v], axis=1), vv, preferred_element_type=F32)
            pv = part if pv is None else pv + part
        if first:
            l_ref[...] = l_add
            acc_ref[...] = pv
        else:
            alpha = jnp.exp2(m_prev - m_new)
            l_ref[...] = alpha * l_ref[...] + l_add
            acc_ref[...] = alpha * acc_ref[...] + pv
        m_ref[...] = m_new

    @pl.when(g == 0)
    def _new_tokens():
        for h in range(n_heads):
            q = q_ref[0, :, h * dv:(h + 1) * dv].astype(F32)
            q_up = jnp.where((row_q < t_new) & (lane_q < dh), q, 0.0)
            q_dn = jnp.where((row_q >= t_new) & (lane_q >= dh), pltpu.roll(q, t_new, 0), 0.0)
            qall_ref[h * rows:(h + 1) * rows, :] = q_up + q_dn
        pad = jnp.zeros((LANE - rows * n_heads, dv), F32)
        kn = jnp.concatenate([kn_ref[0].reshape(rows * n_heads, dv), pad], axis=0)
        vn = jnp.concatenate([vn_ref[0].reshape(rows * n_heads, dv), pad], axis=0)
        r = lax.broadcasted_iota(I32, (n_heads * rows, LANE), 0) & (rows - 1)
        tok_q = jnp.where(r >= t_new, r - t_new, r)
        tok_k = lax.broadcasted_iota(I32, (n_heads * rows, LANE), 1) >> (n_heads.bit_length() - 1)
        accumulate(scores(kn, own_head & (tok_k <= tok_q)), [vn], True)

    blocks = []
    for kr in k_refs:
        blocks += scores(kr[...].reshape(kv_rows, dv), own_head)
    accumulate(blocks, [vr[...].reshape(kv_rows, dv) for vr in v_refs], False)

    @pl.when(g == pl.num_programs(1) - 1)
    def _finish():
        lam = _lambda(lq1_ref, lk1_ref, lq2_ref, lk2_ref, lam_init)
        a_all = acc_ref[...] / l_ref[...]
        for h in range(n_heads):
            a = a_all[h * rows:(h + 1) * rows]
            o = a - lam * pltpu.roll(a, t_new, 0)
            o = _rms(o, gsub_ref[...]) * (1.0 - lam_init)
            o_ref[0, :, h * dv:(h + 1) * dv] = jnp.where(row_q < t_new, o, 0.0)


def _diff_attention_decode(q, k_new, v_new, cache_k, cache_v, layer, page_table, lams, g_sub, lam_init, t_new):
    batch, rows, width = q.shape
    assert rows == 2 * t_new == SUBLANE, "decode kernel stacks the two softmax maps of the new tokens in one sublane tile"
    dv = width // N_DIFF_HEADS
    dh = dv // 2
    n_pages = page_table.shape[1]
    page = cache_k.shape[2]
    n_pg = next(c for c in (16, 8, 4, 2, 1) if n_pages % c == 0)
    steps = n_pages // n_pg
    seq_blk = lambda b, g, pt: (b, 0, 0)
    seq_blk4 = lambda b, g, pt: (b, 0, 0, 0)
    fixed = lambda b, g, pt: (0, 0)

    def page_spec(jj):
        return pl.BlockSpec((None, None, page, N_DIFF_HEADS, dv),
                            lambda b, g, pt: (layer, pt[b * n_pages + g * n_pg + jj], 0, 0, 0))

    grid_spec = pltpu.PrefetchScalarGridSpec(
        num_scalar_prefetch=1,
        grid=(batch, steps),
        in_specs=[pl.BlockSpec((1, rows, width), seq_blk)]
                 + [pl.BlockSpec((1, rows, N_DIFF_HEADS, dv), seq_blk4)] * 2
                 + [pl.BlockSpec((1, dh), fixed)] * 4 + [pl.BlockSpec((1, dv), fixed)]
                 + [page_spec(jj) for jj in range(n_pg)] * 2,
        out_specs=pl.BlockSpec((1, rows, width), seq_blk),
        scratch_shapes=[pltpu.VMEM((N_DIFF_HEADS * rows, 1), F32), pltpu.VMEM((N_DIFF_HEADS * rows, 1), F32),
                        pltpu.VMEM((N_DIFF_HEADS * rows, dv), F32), pltpu.VMEM((N_DIFF_HEADS * rows, dv), F32)],
    )
    return pl.pallas_call(
        functools.partial(_diff_decode_kernel, n_pg=n_pg, t_new=t_new, dh=dh, lam_init=lam_init),
        grid_spec=grid_spec,
        out_shape=jax.ShapeDtypeStruct((batch, rows, width), F32),
        compiler_params=_cparams(("parallel", "arbitrary"), 48),
        name="diff_attention_decode",
    )(page_table.reshape(-1), q, k_new, v_new, *lams, g_sub, *([cache_k] * n_pg), *([cache_v] * n_pg))


def _gla_chunk(q, k, v, g, s_t, chunk):
    nb = chunk // GLA_SUB
    row1 = lax.broadcasted_iota(I32, (chunk, 1), 0)
    b = g * LOG2E
    step = 1
    while step < chunk:
        b = b + jnp.where(row1 >= step, pltpu.roll(b, step, 0), 0.0)
        step *= 2
    b_last = b[chunk - 1:chunk, :]

    qe = (q * jnp.exp2(b)).astype(BF16)
    o = lax.dot_general(qe, s_t.astype(BF16), _NT, preferred_element_type=F32)
    kd = (k * jnp.exp2(b_last - b)).astype(BF16)
    s_new = s_t * jnp.exp2(b_last) + lax.dot_general(v.astype(BF16), kd, _TN, preferred_element_type=F32)

    row = lax.broadcasted_iota(I32, (chunk, chunk), 0)
    col = lax.broadcasted_iota(I32, (chunk, chunk), 1)
    row_in_blk = row1 & (GLA_SUB - 1)

    att = jnp.where(col == row, jnp.sum(q * k, axis=-1, keepdims=True), 0.0)
    for d in range(1, min(GLA_SUB, chunk)):
        k_sh = pltpu.roll(k, d, 0)
        b_sh = pltpu.roll(b, d, 0)
        decay = jnp.exp2(jnp.where(row_in_blk >= d, b - b_sh, -jnp.inf))
        w = jnp.sum(q * k_sh * decay, axis=-1, keepdims=True)
        att = att + jnp.where(col == row - d, w, 0.0)

    dk = q.shape[1]
    for j in range(nb - 1):
        r0 = (j + 1) * GLA_SUB
        b_end = b[r0 - 1:r0, :]
        kj = k[r0 - GLA_SUB:r0] * jnp.exp2(b_end - b[r0 - GLA_SUB:r0])
        above = [jnp.zeros((r0 - GLA_SUB, dk), F32)] if j else []
        kj = jnp.concatenate(above + [kj, jnp.zeros((chunk - r0, dk), F32)], axis=0)
        qj = (q[r0:] * jnp.exp2(b[r0:] - b_end)).astype(BF16)
        a = lax.dot_general(qj, kj.astype(BF16), _NT, preferred_element_type=F32)
        att = att + jnp.concatenate([jnp.zeros((r0, chunk), F32), a], axis=0)

    o = o + jnp.dot(att.astype(BF16), v.astype(BF16), preferred_element_type=F32)
    return o, s_new


def _gla_kernel(gqk_ref, gv_ref, gr_ref, glog_ref, s0_ref, ggla_ref, o_ref, s_ref, st_ref,
                *, chunk, t_valid, dk, dv, scale):
    tb = pl.program_id(1)
    t_blk = gqk_ref.shape[1]
    n_heads = N_GLA_HEADS

    @pl.when(tb == 0)
    def _load_state():
        for h in range(n_heads):
            st_ref[h] = s0_ref[0, h].T

    g_gla = ggla_ref[...]

    def body(c, carry):
        r0 = pl.multiple_of(c * chunk, chunk)
        rows = pl.ds(r0, chunk)
        pos = tb * t_blk + r0 + lax.broadcasted_iota(I32, (chunk, 1), 0)
        live = pos < t_valid
        for h in range(n_heads):
            q = gqk_ref[0, rows, h * dk:(h + 1) * dk] * scale
            k = gqk_ref[0, rows, (n_heads + h) * dk:(n_heads + h + 1) * dk]
            v = gv_ref[0, rows, h * dv:(h + 1) * dv]
            g = jnp.where(live, glog_ref[0, rows, h * dk:(h + 1) * dk], 0.0)
            k = jnp.where(live, k, 0.0)
            o, s_new = _gla_chunk(q, k, v, g, st_ref[h], chunk)
            st_ref[h] = s_new
            gr = gr_ref[0, rows, h * dv:(h + 1) * dv]
            o_ref[0, rows, h * dv:(h + 1) * dv] = (_rms(o, g_gla) * (gr * _sigmoid(gr))).astype(o_ref.dtype)
        return carry

    lax.fori_loop(0, t_blk // chunk, body, 0)

    @pl.when(tb == pl.num_programs(1) - 1)
    def _store_state():
        for h in range(n_heads):
            s_ref[0, h] = st_ref[h].T


def _gla(gqk, gv, gr, glog, s0, g_gla, t_valid, out_dtype):
    batch, seq, _ = gqk.shape
    dk, dv = s0.shape[2], s0.shape[3]
    chunk = next(c for c in (64, 32, 16, 8) if seq % c == 0)
    t_blk = _tile(seq, 256)
    tok = lambda b, t: (b, t, 0)
    return pl.pallas_call(
        functools.partial(_gla_kernel, chunk=chunk, t_valid=t_valid, dk=dk, dv=dv, scale=dk ** -0.5),
        grid=(batch, seq // t_blk),
        in_specs=[
            pl.BlockSpec((1, t_blk, gqk.shape[2]), tok),
            pl.BlockSpec((1, t_blk, gv.shape[2]), tok),
            pl.BlockSpec((1, t_blk, gr.shape[2]), tok),
            pl.BlockSpec((1, t_blk, glog.shape[2]), tok),
            pl.BlockSpec((1, N_GLA_HEADS, dk, dv), lambda b, t: (b, 0, 0, 0)),
            pl.BlockSpec((1, dv), lambda b, t: (0, 0)),
        ],
        out_specs=[
            pl.BlockSpec((1, t_blk, gv.shape[2]), tok),
            pl.BlockSpec((1, N_GLA_HEADS, dk, dv), lambda b, t: (b, 0, 0, 0)),
        ],
        out_shape=[jax.ShapeDtypeStruct(gv.shape, out_dtype), jax.ShapeDtypeStruct(s0.shape, F32)],
        scratch_shapes=[pltpu.VMEM((N_GLA_HEADS, dv, dk), F32)],
        compiler_params=_cparams(("parallel", "arbitrary"), 32),
        name="gla",
    )(gqk, gv, gr, glog, s0, g_gla)


def _cross_kernel(q_ref, mk_ref, mv_ref, o_ref, kh_ref, vh_ref, *, dh):
    @pl.when(pl.program_id(1) == 0)
    def _split_heads():
        for h in range(N_MEM_HEADS):
            kh_ref[h] = mk_ref[0, :, h, :].astype(BF16)
            vh_ref[h] = mv_ref[0, :, h, :].astype(BF16)

    for h in range(N_MEM_HEADS):
        sl = slice(h * dh, (h + 1) * dh)
        s = lax.dot_general(q_ref[0, :, sl].astype(BF16), kh_ref[h], _NT, preferred_element_type=F32)
        pr = jnp.exp(s - jnp.max(s, axis=-1, keepdims=True))
        pr = pr / jnp.sum(pr, axis=-1, keepdims=True)
        o_ref[0, :, sl] = jnp.dot(pr.astype(BF16), vh_ref[h], preferred_element_type=F32).astype(o_ref.dtype)


def _cross_few_rows_kernel(q_ref, mk_ref, mv_ref, o_ref, *, dh):
    rows = q_ref.shape[1]
    mem = mk_ref.shape[1]
    q = jnp.concatenate([q_ref[0, :, h * dh:(h + 1) * dh].astype(F32) for h in range(N_MEM_HEADS)], axis=0)
    k2 = mk_ref[0].reshape(mem * N_MEM_HEADS, dh)
    v2 = mv_ref[0].reshape(mem * N_MEM_HEADS, dh)
    s = lax.dot_general(q, k2, _NT, preferred_element_type=F32)
    own_head = ((lax.broadcasted_iota(I32, s.shape, 1) & (N_MEM_HEADS - 1))
                == lax.broadcasted_iota(I32, s.shape, 0) >> (rows.bit_length() - 1))
    s = jnp.where(own_head, s, -jnp.inf)
    pr = jnp.exp(s - jnp.max(s, axis=-1, keepdims=True))
    pr = pr / jnp.sum(pr, axis=-1, keepdims=True)
    o = jnp.dot(pr, v2, preferred_element_type=F32)
    for h in range(N_MEM_HEADS):
        o_ref[0, :, h * dh:(h + 1) * dh] = o[h * rows:(h + 1) * rows].astype(o_ref.dtype)


def _cross_attention(q, mk, mv, mem_index):
    batch, seq, d = q.shape
    mem, n_heads, dh = mk.shape[-3:]
    tq = _tile(seq, 512)
    lead = mk.ndim - 4
    mem_spec = pl.BlockSpec((None,) * lead + (1, mem, n_heads, dh), lambda b, t: mem_index(b) + (0, 0, 0))
    if seq == SUBLANE:
        return pl.pallas_call(
            functools.partial(_cross_few_rows_kernel, dh=dh),
            grid=(batch, 1),
            in_specs=[pl.BlockSpec((1, seq, d), lambda b, t: (b, 0, 0)), mem_spec, mem_spec],
            out_specs=pl.BlockSpec((1, seq, d), lambda b, t: (b, 0, 0)),
            out_shape=jax.ShapeDtypeStruct(q.shape, F32),
            compiler_params=_cparams(("parallel", "arbitrary"), 40),
            name="cross_attention_few_rows",
        )(q, mk, mv)
    return pl.pallas_call(
        functools.partial(_cross_kernel, dh=dh),
        grid=(batch, seq // tq),
        in_specs=[pl.BlockSpec((1, tq, d), lambda b, t: (b, t, 0)), mem_spec, mem_spec],
        out_specs=pl.BlockSpec((1, tq, d), lambda b, t: (b, t, 0)),
        out_shape=jax.ShapeDtypeStruct(q.shape, BF16 if tq % 16 == 0 else F32),
        scratch_shapes=[pltpu.VMEM((n_heads, mem, dh), BF16), pltpu.VMEM((n_heads, mem, dh), BF16)],
        compiler_params=_cparams(("parallel", "arbitrary"), 40),
        name="cross_attention",
    )(q, mk, mv)


def _two_group_specs(n_a, n_b, tm, d):
    ta = n_a // tm
    return (pl.BlockSpec((tm, d), lambda i, *_: (jnp.minimum(i, ta - 1), 0)),
            pl.BlockSpec((tm, d), lambda i, *_: (jnp.maximum(i - ta, 0), 0)))


def _route_kernel(xa_ref, xb_ref, g_ref, wh_ref, wl_ref, idx_ref, wgt_ref, *, tiles_a):
    x = jnp.where(pl.program_id(0) < tiles_a, xa_ref[...], xb_ref[...])
    xn = _rms(x, g_ref[...])
    xh = xn.astype(BF16)
    xl = (xn - xh.astype(F32)).astype(BF16)
    logits = (jnp.dot(xh, wh_ref[...], preferred_element_type=F32)
              + jnp.dot(xl, wh_ref[...], preferred_element_type=F32)
              + jnp.dot(xh, wl_ref[...], preferred_element_type=F32))
    lane = lax.broadcasted_iota(I32, logits.shape, 1)
    lane_f = lane.astype(F32)
    neg = -jnp.inf

    def first_max(vals):
        top = jnp.max(vals, axis=-1, keepdims=True)
        first = jnp.min(jnp.where(vals == top, lane_f, float(LANE)), axis=-1, keepdims=True)
        return top, first.astype(I32)

    is_grp = lane < N_GROUPS
    g_top, g_idx = first_max(jnp.where(is_grp, logits, neg))
    g_w = 1.0 / jnp.sum(jnp.where(is_grp, jnp.exp(logits - g_top), 0.0), axis=-1, keepdims=True)
    lo = N_GROUPS + g_idx * EXPERTS_PER_GROUP
    le = jnp.where((lane >= lo) & (lane < lo + EXPERTS_PER_GROUP), logits, neg)
    v1, i1 = first_max(le)
    v2, i2 = first_max(jnp.where(lane == i1, neg, le))
    e2 = jnp.exp(v2 - v1)
    w1 = g_w / (1.0 + e2)
    w2 = g_w * e2 / (1.0 + e2)
    idx_ref[...] = jnp.where(lane == 0, i1 - N_GROUPS, jnp.where(lane == 1, i2 - N_GROUPS, 0))
    wgt_ref[...] = jnp.where(lane == 0, w1, jnp.where(lane == 1, w2, 0.0))


def _route(xa, xb, tm, g_ffn, w_hi, w_lo):
    d = xa.shape[1]
    n = xa.shape[0] + xb.shape[0]
    row = lambda i: (i, 0)
    fixed = lambda i: (0, 0)
    return pl.pallas_call(
        functools.partial(_route_kernel, tiles_a=xa.shape[0] // tm),
        grid=(n // tm,),
        in_specs=[*_two_group_specs(xa.shape[0], xb.shape[0], tm, d), pl.BlockSpec((1, d), fixed),
                  pl.BlockSpec((d, LANE), fixed), pl.BlockSpec((d, LANE), fixed)],
        out_specs=[pl.BlockSpec((tm, LANE), row), pl.BlockSpec((tm, LANE), row)],
        out_shape=[jax.ShapeDtypeStruct((n, LANE), I32), jax.ShapeDtypeStruct((n, LANE), F32)],
        compiler_params=_cparams(("parallel",), 40),
        name="moe_route",
    )(xa, xb, g_ffn, w_hi, w_lo)


def _start_row_copies(n_rows, make_copy):
    def start(it, c):
        for u in range(DMA_UNROLL):
            for kk in range(2):
                make_copy(it * DMA_UNROLL + u, kk).start(priority=kk)
        return c

    assert n_rows % DMA_UNROLL == 0
    lax.fori_loop(0, n_rows // DMA_UNROLL, start, 0)


def _wait_row_copies(n_rows, make_copy):
    def wait(it, c):
        for u in range(DMA_UNROLL):
            for kk in range(2):
                make_copy(it * DMA_UNROLL + u, kk).wait()
        return c

    lax.fori_loop(0, n_rows // DMA_UNROLL, wait, 0)


def _dispatch_kernel(pos_ref, pend_ref, xa_ref, xb_ref, xs_ref, zero_ref, sem, zsem, *, tile_e, tm, tiles_a):
    i = pl.program_id(0)

    @pl.when(i == 0)
    def _zero_padding():
        zero_ref[...] = jnp.zeros(zero_ref.shape, F32)

        def fill(e):
            start = pl.multiple_of(jnp.maximum(pend_ref[e] - tile_e, 0), tile_e)
            return pltpu.make_async_copy(zero_ref, xs_ref.at[pl.ds(start, tile_e), :], zsem)

        for e in range(N_EXPERTS):
            fill(e).start()
        for e in range(N_EXPERTS):
            fill(e).wait()

        def tail(t):
            return pltpu.make_async_copy(zero_ref, xs_ref.at[pl.ds(pl.multiple_of(t * tile_e, tile_e), tile_e), :], zsem)

        first_unused = pend_ref[N_EXPERTS - 1] // tile_e
        n_tiles = xs_ref.shape[0] // tile_e
        lax.fori_loop(first_unused, n_tiles, lambda t, c: (tail(t).start(), c)[1], 0)
        lax.fori_loop(first_unused, n_tiles, lambda t, c: (tail(t).wait(), c)[1], 0)

    def scatter_from(x_ref):
        def row_copy(r, kk):
            dst = pos_ref[(i * tm + r) * 2 + kk]
            return pltpu.make_async_copy(x_ref.at[pl.ds(r, 1), :], xs_ref.at[pl.ds(dst, 1), :], sem.at[kk])

        _start_row_copies(tm, row_copy)
        _wait_row_copies(tm, row_copy)

    @pl.when(i < tiles_a)
    def _():
        scatter_from(xa_ref)

    @pl.when(i >= tiles_a)
    def _():
        scatter_from(xb_ref)


def _dispatch(xa, xb, tm, pos, p_end, n_slots, tile_e):
    d = xa.shape[1]
    n = xa.shape[0] + xb.shape[0]
    grid_spec = pltpu.PrefetchScalarGridSpec(
        num_scalar_prefetch=2,
        grid=(n // tm,),
        in_specs=list(_two_group_specs(xa.shape[0], xb.shape[0], tm, d)),
        out_specs=pl.BlockSpec(memory_space=pl.ANY),
        scratch_shapes=[pltpu.VMEM((tile_e, d), F32), pltpu.SemaphoreType.DMA((2,)), pltpu.SemaphoreType.DMA(())],
    )
    return pl.pallas_call(
        functools.partial(_dispatch_kernel, tile_e=tile_e, tm=tm, tiles_a=xa.shape[0] // tm),
        grid_spec=grid_spec,
        out_shape=jax.ShapeDtypeStruct((n_slots, d), F32),
        compiler_params=_cparams(("arbitrary",), 32),
        name="moe_dispatch",
    )(pos, p_end, xa, xb)


def _experts_kernel(te_ref, nu_ref, first_ref, next_ref, slot_ref, x_ref, g_ref, wg_hbm, wu_hbm, wd_hbm, y_ref,
                    wgf_ref, wuf_ref, wdf_ref, wgb_ref, wub_ref, wdb_ref, sem):
    t = pl.program_id(0)

    def fetch(e, slot):
        return [pltpu.make_async_copy(src.at[0, e], dst.at[slot], sem.at[slot, j])
                for j, (src, dst) in enumerate(((wg_hbm, wgf_ref), (wu_hbm, wuf_ref), (wd_hbm, wdf_ref)))]

    @pl.when(t == 0)
    def _first_fetch():
        for c in fetch(te_ref[0], slot_ref[0]):
            c.start()

    @pl.when(first_ref[t] == 1)
    def _new_expert():
        slot = slot_ref[t]
        for c in fetch(te_ref[t], slot):
            c.wait()

        @pl.when(next_ref[t] >= 0)
        def _prefetch():
            for c in fetch(next_ref[t], 1 - slot):
                c.start()

        wgb_ref[...] = wgf_ref[slot].astype(BF16)
        wub_ref[...] = wuf_ref[slot].astype(BF16)
        wdb_ref[...] = wdf_ref[slot].astype(BF16)

    @pl.when(t < nu_ref[0])
    def _mlp():
        x = _rms(x_ref[...], g_ref[...]).astype(BF16)
        hg = jnp.dot(x, wgb_ref[...], preferred_element_type=F32)
        hu = jnp.dot(x, wub_ref[...], preferred_element_type=F32)
        h = (hg * _sigmoid(hg)) * hu
        y_ref[...] = jnp.dot(h.astype(BF16), wdb_ref[...], preferred_element_type=F32)

    @pl.when(t >= nu_ref[0])
    def _unused():
        y_ref[...] = jnp.zeros(y_ref.shape, F32)


def _experts(xs, g_ffn, tile_expert, n_used, w_gate, w_up, w_down, tm):
    n_slots, d = xs.shape
    de = w_gate.shape[-1]
    assert w_gate.ndim == 4 and w_gate.shape[0] == 1, "weights come as (1, experts, rows, cols)"
    n_tiles = n_slots // tm
    idx = jnp.arange(n_tiles, dtype=I32)
    first = jnp.concatenate([jnp.ones((1,), I32), (tile_expert[1:] != tile_expert[:-1]).astype(I32)])
    slot = (jnp.cumsum(first) - 1) & 1
    run_start = jnp.where(first == 1, idx, n_tiles)
    next_start = jnp.concatenate([lax.cummin(run_start, reverse=True)[1:], jnp.full((1,), n_tiles, I32)])
    next_expert = jnp.where(next_start < n_tiles, tile_expert[jnp.minimum(next_start, n_tiles - 1)], -1)
    any_spec = pl.BlockSpec(memory_space=pl.ANY)
    grid_spec = pltpu.PrefetchScalarGridSpec(
        num_scalar_prefetch=5,
        grid=(n_tiles,),
        in_specs=[pl.BlockSpec((tm, d), lambda t, te, nu, *_: (jnp.minimum(t, nu[0] - 1), 0)),
                  pl.BlockSpec((1, d), lambda t, *_: (0, 0)), any_spec, any_spec, any_spec],
        out_specs=pl.BlockSpec((tm, d), lambda t, *_: (t, 0)),
        scratch_shapes=[pltpu.VMEM((2, d, de), F32), pltpu.VMEM((2, d, de), F32), pltpu.VMEM((2, de, d), F32),
                        pltpu.VMEM((d, de), BF16), pltpu.VMEM((d, de), BF16), pltpu.VMEM((de, d), BF16),
                        pltpu.SemaphoreType.DMA((2, 3))],
    )
    return pl.pallas_call(
        _experts_kernel,
        grid_spec=grid_spec,
        out_shape=jax.ShapeDtypeStruct((n_slots, d), F32),
        compiler_params=_cparams(("arbitrary",), 52),
        name="moe_experts",
    )(tile_expert, n_used, first, next_expert.astype(I32), slot.astype(I32), xs, g_ffn, w_gate, w_up, w_down)


def _combine_kernel(pos_ref, xa_ref, xb_ref, wgt_ref, g_ref, ys_ref, oa_ref, ob_ref, buf_ref, sem, *, tiles_a):
    i = pl.program_id(0)
    tm = xa_ref.shape[0]

    def gather(step):
        slot = step & 1

        def row_copy(r, kk):
            src = pos_ref[(step * tm + r) * 2 + kk]
            return pltpu.make_async_copy(ys_ref.at[pl.ds(src, 1), :], buf_ref.at[slot, kk, pl.ds(r, 1), :],
                                         sem.at[slot, kk])
        return row_copy

    @pl.when(i == 0)
    def _():
        _start_row_copies(tm, gather(i))

    @pl.when(i + 1 < pl.num_programs(0))
    def _():
        _start_row_copies(tm, gather(i + 1))

    _wait_row_copies(tm, gather(i))
    slot = i & 1
    wgt = wgt_ref[...]
    first_group = i < tiles_a
    x = jnp.where(first_group, xa_ref[...], xb_ref[...])
    y = _rms(x + (wgt[:, 0:1] * buf_ref[slot, 0] + wgt[:, 1:2] * buf_ref[slot, 1]), g_ref[...])

    @pl.when(first_group)
    def _():
        oa_ref[...] = y

    @pl.when(jnp.logical_not(first_group))
    def _():
        ob_ref[...] = y


def _combine(xa, xb, tm, wgt, g_final, ys, pos):
    d = xa.shape[1]
    n = xa.shape[0] + xb.shape[0]
    grid_spec = pltpu.PrefetchScalarGridSpec(
        num_scalar_prefetch=1,
        grid=(n // tm,),
        in_specs=[*_two_group_specs(xa.shape[0], xb.shape[0], tm, d), pl.BlockSpec((tm, LANE), lambda i, pos: (i, 0)),
                  pl.BlockSpec((1, d), lambda i, pos: (0, 0)), pl.BlockSpec(memory_space=pl.ANY)],
        out_specs=list(_two_group_specs(xa.shape[0], xb.shape[0], tm, d)),
        scratch_shapes=[pltpu.VMEM((2, 2, tm, d), F32), pltpu.SemaphoreType.DMA((2, 2))],
    )
    return pl.pallas_call(
        functools.partial(_combine_kernel, tiles_a=xa.shape[0] // tm),
        grid_spec=grid_spec,
        out_shape=[jax.ShapeDtypeStruct(xa.shape, F32), jax.ShapeDtypeStruct(xb.shape, F32)],
        compiler_params=_cparams(("arbitrary",), 40),
        name="moe_combine",
    )(pos, xa, xb, wgt, g_final, ys)


def _moe_and_final_norm(xa, xb, g_ffn, w_route_hi, w_route_lo, w_gate, w_up, w_down, g_final):
    d = xa.shape[1]
    n = xa.shape[0] + xb.shape[0]
    tm_tok = _tile(math.gcd(xa.shape[0], xb.shape[0]), 256)
    assert tm_tok % DMA_UNROLL == 0
    ridx, rw = _route(xa, xb, tm_tok, g_ffn, w_route_hi, w_route_lo)
    tm = 256 if n >= 2048 else 128
    e = ridx[:, :2].reshape(-1)
    onehot = (e[:, None] == jnp.arange(N_EXPERTS, dtype=I32)[None, :]).astype(I32)
    csum = jnp.cumsum(onehot, axis=0)
    counts = csum[-1]
    padded = (counts + tm - 1) // tm * tm
    p_end = jnp.cumsum(padded).astype(I32)
    pos = jnp.sum(onehot * ((p_end - padded)[None, :] + csum - 1), axis=1).astype(I32)
    n_tiles = (2 * n + N_EXPERTS * (tm - 1)) // tm + 1
    tile_start = jnp.arange(n_tiles, dtype=I32) * tm
    tile_expert = jnp.minimum(jnp.sum((tile_start[:, None] >= p_end[None, :]).astype(I32), axis=1), N_EXPERTS - 1)
    n_used = (p_end[-1:] // tm).astype(I32)
    xs = _dispatch(xa, xb, tm_tok, pos, p_end, n_tiles * tm, tm)
    ys = _experts(xs, g_ffn, tile_expert.astype(I32), n_used, w_gate, w_up, w_down, tm)
    return _combine(xa, xb, tm_tok, rw, g_final, ys, pos)


def _layer(x3, s0, t_valid, mk, mv, mem_index, diff_fn, lw):
    batch, seq, d = x3.shape
    n = batch * seq
    x = x3.reshape(n, d)
    qs, k4, v4, kb, vb, gqk, gv, gr, glog = _in_projection(x, lw["g_mix"], lw["w_in_main"], lw["w_in_ga"],
                                                            lw["w_a2p"], lw["b_a"])
    d_out = diff_fn(qs, k4, v4, kb, vb)
    r3 = lambda a: a.reshape(batch, seq, a.shape[-1])
    g_out, s_new = _gla(r3(gqk), r3(gv), r3(gr), r3(glog), s0, lw["g_gla"], t_valid,
                        BF16 if seq % 16 == 0 else F32)
    x1 = _matmul_residual([d_out.reshape(n, -1), g_out.reshape(n, -1)], lw["w_o"], x, "mixer_out")
    dh_mem = d // N_MEM_HEADS
    xq = _norm_matmul(x1, lw["g_cross"], lw["w_q_mem"], BF16 if seq % 16 == 0 else F32, "cross_q",
                      out_scale=dh_mem ** -0.5)
    ca = _cross_attention(xq.reshape(batch, seq, d), mk, mv, mem_index)
    x2 = _matmul_residual([ca.reshape(n, d)], lw["w_o_mem"], x1, "cross_out")
    return x2, k4, v4, s_new


def kernel(x_prompt, x_sample, mem_prompt, cache_k, cache_v, state_gla, cache_mem_k, cache_mem_v, page_table, g_mix, w_in, w_a2, b_a, lambda_q1, lambda_k1, lambda_q2, lambda_k2, g_subln, g_gla, w_o, g_mem, g_cross, w_q_mem, w_k_mem, w_v_mem, w_o_mem, g_ffn, w_group, w_router, w_gate, w_up, w_down, g_final):
    depth = w_in.shape[0]
    batch, seq, d = x_prompt.shape
    dec_batch, dec_seq, _ = x_sample.shape
    mem_len = mem_prompt.shape[1]
    width = d // 2
    main_cols = w_in.shape[2] - GATE_RANK
    assert main_cols == 6 * width and w_a2.shape[1:] == (GATE_RANK, width // 2)
    dk_gla = width // N_GLA_HEADS // 2
    dv_gla = width // N_GLA_HEADS
    dv_diff = width // N_DIFF_HEADS
    dec_pad = -(-dec_seq // SUBLANE) * SUBLANE

    xp = x_prompt
    xs = jnp.pad(x_sample, ((0, 0), (0, dec_pad - dec_seq), (0, 0)))
    s0_prompt = jnp.zeros((batch, N_GLA_HEADS, dk_gla, dv_gla), F32)
    row2 = lambda a: a.reshape(1, -1)
    outs = {name: [] for name in ("kp", "vp", "sp", "mkp", "mvp", "ks", "vs", "ss")}
    for l in range(depth):
        lam_init = 0.8 - 0.6 * math.exp(-0.3 * l)
        lw = dict(
            g_mix=row2(g_mix[l]), g_gla=row2(g_gla[l]), g_cross=row2(g_cross[l]), b_a=row2(b_a[l]),
            w_in_main=w_in[l][:, :main_cols].astype(BF16),
            w_in_ga=jnp.pad(w_in[l][:, main_cols:], ((0, 0), (0, LANE - GATE_RANK))).astype(BF16),
            w_a2p=jnp.pad(w_a2[l], ((0, LANE - GATE_RANK), (0, 0))).astype(BF16),
            w_o=w_o[l].astype(BF16), w_q_mem=w_q_mem[l].astype(BF16), w_o_mem=w_o_mem[l].astype(BF16),
        )
        lams = (row2(lambda_q1[l]), row2(lambda_k1[l]), row2(lambda_q2[l]), row2(lambda_k2[l]))
        g_sub = row2(g_subln[l])
        w_route = jnp.pad(jnp.concatenate([w_group[l], w_router[l]], axis=1),
                          ((0, 0), (0, LANE - N_GROUPS - N_EXPERTS)))
        w_route_hi = w_route.astype(BF16)
        w_route_lo = (w_route - w_route_hi.astype(F32)).astype(BF16)

        mem2 = mem_prompt.reshape(batch * mem_len, d)
        mkp = _norm_matmul(mem2, row2(g_mem[l]), w_k_mem[l].astype(BF16), F32, "mem_k", n_heads=N_MEM_HEADS)
        mvp = _norm_matmul(mem2, row2(g_mem[l]), w_v_mem[l].astype(BF16), F32, "mem_v", n_heads=N_MEM_HEADS)
        mkp = mkp.reshape(batch, mem_len, N_MEM_HEADS, -1)
        mvp = mvp.reshape(batch, mem_len, N_MEM_HEADS, -1)

        prompt_attn = lambda qs, k4, v4, kb, vb: _diff_attention_prompt(qs, kb, vb, lams, g_sub, lam_init, batch, seq,
                                                                        BF16)
        xp2, kp, vp, sp = _layer(xp, s0_prompt, seq, mkp, mvp, lambda b: (b,), prompt_attn, lw)

        sample_attn = lambda qs, k4, v4, kb, vb, l=l: _diff_attention_decode(
            qs.reshape(dec_batch, dec_pad, width), k4.reshape(dec_batch, dec_pad, N_DIFF_HEADS, dv_diff),
            v4.reshape(dec_batch, dec_pad, N_DIFF_HEADS, dv_diff), cache_k, cache_v, l, page_table, lams, g_sub,
            lam_init, dec_seq)
        xs2, ks, vs, ss = _layer(xs, state_gla[l], dec_seq, cache_mem_k, cache_mem_v, lambda b, l=l: (l, b),
                                 sample_attn, lw)

        assert l == depth - 1, "the MoE combine kernel fuses the final norm; deeper stacks need an un-normed variant"
        moe = functools.partial(_moe_and_final_norm, g_ffn=row2(g_ffn[l]), w_route_hi=w_route_hi,
                                w_route_lo=w_route_lo, w_gate=w_gate[l:l + 1], w_up=w_up[l:l + 1],
                                w_down=w_down[l:l + 1], g_final=row2(g_final))
        xp, xs = moe(xp2, xs2)
        xp = xp.reshape(batch, seq, d)
        xs = xs.reshape(dec_batch, dec_pad, d)

        outs["kp"].append(kp.reshape(batch, seq, N_DIFF_HEADS, dv_diff))
        outs["vp"].append(vp.reshape(batch, seq, N_DIFF_HEADS, dv_diff))
        outs["sp"].append(sp)
        outs["mkp"].append(mkp)
        outs["mvp"].append(mvp)
        outs["ks"].append(ks.reshape(dec_batch, dec_pad, N_DIFF_HEADS, dv_diff)[:, :dec_seq])
        outs["vs"].append(vs.reshape(dec_batch, dec_pad, N_DIFF_HEADS, dv_diff)[:, :dec_seq])
        outs["ss"].append(ss)

    st = lambda name: jnp.stack(outs[name])
    return (xp, xs[:, :dec_seq], st("kp"), st("vp"), st("sp"), st("mkp"), st("mvp"), st("ks"), st("vs"), st("ss"))
```
